```python
import math
import jax, jax.numpy as jnp
from jax import lax
import numpy as np

D_MODEL = 1024
BATCH = 8
SEQ = 4096
DEPTH = 2

HEAD_DIM = 64
BLOCK = 128
NEG_INF = -1e30
A_HEADS = D_MODEL // (2 * HEAD_DIM)
A_KV_HEADS = 2
A_WINDOW = 128
A_Q = A_HEADS * HEAD_DIM
A_KV = A_KV_HEADS * HEAD_DIM
B_HEADS = D_MODEL // (2 * HEAD_DIM)
B_Q_RANK = 3 * D_MODEL // 8
B_KV_RANK = D_MODEL // 4
B_NOPE_DIM = 64
B_ROPE_DIM = 32
B_V_DIM = 64
ROPE_BASE = 10000.0
C_HEADS = D_MODEL // HEAD_DIM
GRID_W = 64
C_WIN_H = 8
C_WIN_W = 16
N_EXPERTS = 16
EC_CAPACITY_FACTOR = 2
D_EXPERT = 2 * D_MODEL
LN_EPS = 1e-5
RMS_EPS = 1e-6
DN_ALPHA = (2 * DEPTH) ** 0.25
DN_BETA = (8 * DEPTH) ** -0.25

EVEN_IN_WIDTH = A_Q + 2 * A_KV + B_Q_RANK + B_KV_RANK + B_ROPE_DIM
EVEN_MIX_WIDTH = A_Q + B_HEADS * B_V_DIM
ODD_MIX_WIDTH = C_HEADS * HEAD_DIM

kernel_name = "hybrid_swa_mla_natten_ecmoe_encoder"


def layer_norm(x, g, b):
    xf = x.astype(jnp.float32)
    mu = jnp.mean(xf, -1, keepdims=True)
    var = jnp.mean(jnp.square(xf - mu), -1, keepdims=True)
    return ((xf - mu) * lax.rsqrt(var + LN_EPS) * g.astype(jnp.float32) + b.astype(jnp.float32)).astype(x.dtype)


def rms_norm(x, g):
    xf = x.astype(jnp.float32)
    return (xf * lax.rsqrt(jnp.mean(jnp.square(xf), -1, keepdims=True) + RMS_EPS) * g.astype(jnp.float32)).astype(x.dtype)


def rope(x, pos):
    half = x.shape[-1] // 2
    inv = ROPE_BASE ** (-jnp.arange(half, dtype=jnp.float32) / half)
    ang = pos.astype(jnp.float32)[:, None] * inv[None, :]
    cos = jnp.cos(ang)[None, :, None, :]
    sin = jnp.sin(ang)[None, :, None, :]
    xf = x.astype(jnp.float32)
    x1, x2 = xf[..., :half], xf[..., half:]
    return jnp.concatenate([x1 * cos - x2 * sin, x1 * sin + x2 * cos], -1).astype(x.dtype)


def alibi_slopes(n):
    return 2.0 ** (-8.0 * (jnp.arange(n, dtype=jnp.float32) + 1.0) / n)


def windowed_gqa_sink(q, k, v, sink):
    B, S = q.shape[0], q.shape[1]
    nb = S // BLOCK
    G = A_KV_HEADS
    R = A_HEADS // G
    q = q.reshape(B, S, G, R, HEAD_DIM)
    pad = ((0, 0), (A_WINDOW, A_WINDOW), (0, 0), (0, 0))
    kp = jnp.pad(k, pad)
    vp = jnp.pad(v, pad)
    span = BLOCK + 2 * A_WINDOW
    slopes = alibi_slopes(A_HEADS).reshape(G, R)
    sink_gr = sink.astype(jnp.float32).reshape(G, R)
    scale = HEAD_DIM ** -0.5

    def one_block(i):
        start = i * BLOCK
        qb = lax.dynamic_slice_in_dim(q, start, BLOCK, axis=1)
        kb = lax.dynamic_slice_in_dim(kp, start, span, axis=1)
        vb = lax.dynamic_slice_in_dim(vp, start, span, axis=1)
        s = jnp.einsum('bqgrd,bkgd->bgrqk', qb, kb, preferred_element_type=jnp.float32) * scale
        t = start + jnp.arange(BLOCK)
        src = start - A_WINDOW + jnp.arange(span)
        dist = jnp.abs(t[:, None] - src[None, :])
        valid = (dist <= A_WINDOW) & (src[None, :] >= 0) & (src[None, :] < S)
        s = s - slopes[:, :, None, None] * dist.astype(jnp.float32)
        s = jnp.where(valid, s, NEG_INF)
        m = jnp.maximum(jnp.max(s, -1), sink_gr[:, :, None])
        p = jnp.exp(s - m[..., None])
        denom = jnp.sum(p, -1) + jnp.exp(sink_gr[:, :, None] - m)
        o = jnp.einsum('bgrqk,bkgd->bqgrd', p, vb.astype(jnp.float32))
        o = o / jnp.transpose(denom, (0, 3, 1, 2))[..., None]
        return o.reshape(B, BLOCK, A_HEADS * HEAD_DIM).astype(k.dtype)

    out = lax.map(one_block, jnp.arange(nb))
    return jnp.transpose(out, (1, 0, 2, 3)).reshape(B, S, A_Q)


def mla(c_q, c_kv, k_rope, pos, q_norm_g, w_q_up, kv_norm_g, w_kv_up):
    B, S = c_q.shape[0], c_q.shape[1]
    nb = S // BLOCK
    q = (rms_norm(c_q, q_norm_g) @ w_q_up).reshape(B, S, B_HEADS, B_NOPE_DIM + B_ROPE_DIM)
    q_nope = q[..., :B_NOPE_DIM]
    q_pe = rope(q[..., B_NOPE_DIM:], pos)
    kv = (rms_norm(c_kv, kv_norm_g) @ w_kv_up).reshape(B, S, B_HEADS, B_NOPE_DIM + B_V_DIM)
    k_nope = kv[..., :B_NOPE_DIM]
    v = kv[..., B_NOPE_DIM:]
    k_pe = rope(k_rope[:, :, None, :], pos)[:, :, 0, :]
    scale = (B_NOPE_DIM + B_ROPE_DIM) ** -0.5

    def one_block(i):
        start = i * BLOCK
        qn = lax.dynamic_slice_in_dim(q_nope, start, BLOCK, axis=1)
        qp = lax.dynamic_slice_in_dim(q_pe, start, BLOCK, axis=1)
        s = (jnp.einsum('bqhd,bkhd->bhqk', qn, k_nope, preferred_element_type=jnp.float32)
             + jnp.einsum('bqhd,bkd->bhqk', qp, k_pe, preferred_element_type=jnp.float32)) * scale
        p = jax.nn.softmax(s, axis=-1)
        o = jnp.einsum('bhqk,bkhd->bqhd', p, v.astype(jnp.float32))
        return o.reshape(B, BLOCK, B_HEADS * B_V_DIM).astype(c_q.dtype)

    out = lax.map(one_block, jnp.arange(nb))
    return jnp.transpose(out, (1, 0, 2, 3)).reshape(B, S, B_HEADS * B_V_DIM)


def neighbourhood_attention(q, k, v, rpb):
    B, S = q.shape[0], q.shape[1]
    rows = S // GRID_W
    wh = min(C_WIN_H, rows)
    qg = q.reshape(B, rows, GRID_W, C_HEADS, HEAD_DIM)
    kg = k.reshape(B, rows, GRID_W, C_HEADS, HEAD_DIM)
    vg = v.reshape(B, rows, GRID_W, C_HEADS, HEAD_DIM)
    cols = jnp.arange(GRID_W)
    col_start = jnp.clip(cols - C_WIN_W // 2, 0, GRID_W - C_WIN_W)
    col_valid = (cols[None, :] >= col_start[:, None]) & (cols[None, :] < col_start[:, None] + C_WIN_W)
    dc_idx = jnp.clip(cols[None, :] - cols[:, None] + C_WIN_W - 1, 0, 2 * C_WIN_W - 2)
    scale = HEAD_DIM ** -0.5

    def one_row(r):
        rs = jnp.clip(r - wh // 2, 0, rows - wh)
        qr = lax.dynamic_index_in_dim(qg, r, axis=1, keepdims=False)
        kb = lax.dynamic_slice_in_dim(kg, rs, wh, axis=1)
        vb = lax.dynamic_slice_in_dim(vg, rs, wh, axis=1)
        s = jnp.einsum('bchd,bwxhd->bhcwx', qr, kb, preferred_element_type=jnp.float32) * scale
        dr_idx = rs + jnp.arange(wh) - r + C_WIN_H - 1
        bias = rpb[:, dr_idx[:, None, None], dc_idx[None, :, :]]
        s = s + jnp.transpose(bias, (0, 2, 1, 3))[None].astype(jnp.float32)
        s = jnp.where(col_valid[None, None, :, None, :], s, NEG_INF)
        p = jax.nn.softmax(s.reshape(B, C_HEADS, GRID_W, wh * GRID_W), axis=-1).reshape(s.shape)
        o = jnp.einsum('bhcwx,bwxhd->bchd', p, vb.astype(jnp.float32))
        return o.reshape(B, GRID_W, C_HEADS * HEAD_DIM).astype(q.dtype)

    out = lax.map(one_row, jnp.arange(rows))
    return jnp.transpose(out, (1, 0, 2, 3)).reshape(B, S, C_HEADS * HEAD_DIM)


def expert_choice_moe(h, w_router, w_gate, w_up, w_down):
    B, S, D = h.shape
    cap = EC_CAPACITY_FACTOR * S // N_EXPERTS
    logits = jnp.einsum('bsd,de->bse', h, w_router, preferred_element_type=jnp.float32)
    aff = jax.nn.softmax(logits, axis=-1)
    g, idx = lax.top_k(jnp.transpose(aff, (0, 2, 1)), cap)
    xin = jax.vmap(lambda hb, ib: hb[ib])(h, idx)
    hid = jax.nn.silu(jnp.einsum('becd,edf->becf', xin, w_gate)) * jnp.einsum('becd,edf->becf', xin, w_up)
    y = jnp.einsum('becf,efd->becd', hid, w_down) * g[..., None].astype(h.dtype)
    return jax.vmap(lambda yb, ib: jnp.zeros((S, D), yb.dtype).at[ib.reshape(-1)].add(yb.reshape(-1, D)))(y, idx)


def even_mixer(x, w_in, a_sink, mla_q_norm, w_q_up, mla_kv_norm, w_kv_up, w_out):
    B, S, _ = x.shape
    pos = jnp.arange(S, dtype=jnp.int32)
    proj = x @ w_in
    o1 = A_Q
    o2 = o1 + A_KV
    o3 = o2 + A_KV
    o4 = o3 + B_Q_RANK
    o5 = o4 + B_KV_RANK
    qa, ka, va = proj[..., :o1], proj[..., o1:o2], proj[..., o2:o3]
    c_q, c_kv, k_rope = proj[..., o3:o4], proj[..., o4:o5], proj[..., o5:]
    out_a = windowed_gqa_sink(qa.reshape(B, S, A_HEADS, HEAD_DIM),
                              ka.reshape(B, S, A_KV_HEADS, HEAD_DIM),
                              va.reshape(B, S, A_KV_HEADS, HEAD_DIM), a_sink)
    out_b = mla(c_q, c_kv, k_rope, pos, mla_q_norm, w_q_up, mla_kv_norm, w_kv_up)
    return jnp.concatenate([out_a, out_b], axis=-1) @ w_out


def odd_mixer(x, w_qkv, na_rpb, w_out):
    B, S, _ = x.shape
    qkv = x @ w_qkv
    q = qkv[..., :ODD_MIX_WIDTH].reshape(B, S, C_HEADS, HEAD_DIM)
    k = qkv[..., ODD_MIX_WIDTH:2 * ODD_MIX_WIDTH].reshape(B, S, C_HEADS, HEAD_DIM)
    v = qkv[..., 2 * ODD_MIX_WIDTH:].reshape(B, S, C_HEADS, HEAD_DIM)
    return neighbourhood_attention(q, k, v, na_rpb) @ w_out


def setup_inputs(seed: int = 0) -> dict:
    key = jax.random.key(seed)
    ks = jax.random.split(key, 32)
    f32 = jnp.float32

    def nrm(k, shape, scale):
        return jax.random.normal(k, shape, f32) * scale

    D = D_MODEL
    return {
        "x": nrm(ks[0], (BATCH, SEQ, D), 1.0),
        "w_in0": nrm(ks[1], (D, EVEN_IN_WIDTH), D ** -0.5),
        "a_sink": nrm(ks[2], (A_HEADS,), 0.5),
        "mla_q_norm": 1.0 + nrm(ks[3], (B_Q_RANK,), 0.01),
        "w_q_up": nrm(ks[4], (B_Q_RANK, B_HEADS * (B_NOPE_DIM + B_ROPE_DIM)), B_Q_RANK ** -0.5),
        "mla_kv_norm": 1.0 + nrm(ks[5], (B_KV_RANK,), 0.01),
        "w_kv_up": nrm(ks[6], (B_KV_RANK, B_HEADS * (B_NOPE_DIM + B_V_DIM)), B_KV_RANK ** -0.5),
        "w_out0": nrm(ks[7], (EVEN_MIX_WIDTH, D), DN_BETA * EVEN_MIX_WIDTH ** -0.5),
        "ln0a_g": 1.0 + nrm(ks[8], (D,), 0.01),
        "ln0a_b": nrm(ks[9], (D,), 0.01),
        "router0": nrm(ks[10], (D, N_EXPERTS), D ** -0.5),
        "w_gate0": nrm(ks[11], (N_EXPERTS, D, D_EXPERT), D ** -0.5),
        "w_up0": nrm(ks[12], (N_EXPERTS, D, D_EXPERT), D ** -0.5),
        "w_down0": nrm(ks[13], (N_EXPERTS, D_EXPERT, D), DN_BETA * D_EXPERT ** -0.5),
        "ln0b_g": 1.0 + nrm(ks[14], (D,), 0.01),
        "ln0b_b": nrm(ks[15], (D,), 0.01),
        "w_qkv1": nrm(ks[16], (D, 3 * ODD_MIX_WIDTH), D ** -0.5),
        "na_rpb": nrm(ks[17], (C_HEADS, 2 * C_WIN_H - 1, 2 * C_WIN_W - 1), 0.02),
        "w_out1": nrm(ks[18], (ODD_MIX_WIDTH, D), DN_BETA * ODD_MIX_WIDTH ** -0.5),
        "ln1a_g": 1.0 + nrm(ks[19], (D,), 0.01),
        "ln1a_b": nrm(ks[20], (D,), 0.01),
        "router1": nrm(ks[21], (D, N_EXPERTS), D ** -0.5),
        "w_gate1": nrm(ks[22], (N_EXPERTS, D, D_EXPERT), D ** -0.5),
        "w_up1": nrm(ks[23], (N_EXPERTS, D, D_EXPERT), D ** -0.5),
        "w_down1": nrm(ks[24], (N_EXPERTS, D_EXPERT, D), DN_BETA * D_EXPERT ** -0.5),
        "ln1b_g": 1.0 + nrm(ks[25], (D,), 0.01),
        "ln1b_b": nrm(ks[26], (D,), 0.01),
    }


def reference(x, w_in0, a_sink, mla_q_norm, w_q_up, mla_kv_norm, w_kv_up, w_out0, ln0a_g, ln0a_b,
              router0, w_gate0, w_up0, w_down0, ln0b_g, ln0b_b,
              w_qkv1, na_rpb, w_out1, ln1a_g, ln1a_b,
              router1, w_gate1, w_up1, w_down1, ln1b_g, ln1b_b):
    even_params = [(w_in0, a_sink, mla_q_norm, w_q_up, mla_kv_norm, w_kv_up, w_out0)]
    odd_params = [(w_qkv1, na_rpb, w_out1)]
    mix_norms = [(ln0a_g, ln0a_b), (ln1a_g, ln1a_b)]
    moe_params = [(router0, w_gate0, w_up0, w_down0), (router1, w_gate1, w_up1, w_down1)]
    moe_norms = [(ln0b_g, ln0b_b), (ln1b_g, ln1b_b)]
    h = x
    for layer in range(DEPTH):
        if layer % 2 == 0:
            mix = even_mixer(h, *even_params[layer // 2])
        else:
            mix = odd_mixer(h, *odd_params[layer // 2])
        h = layer_norm(DN_ALPHA * h + mix, *mix_norms[layer])
        h = layer_norm(DN_ALPHA * h + expert_choice_moe(h, *moe_params[layer]), *moe_norms[layer])
    return h
```

```python
import functools

import jax
import jax.numpy as jnp
from jax import lax
from jax.experimental import pallas as pl
from jax.experimental.pallas import tpu as pltpu

F32 = jnp.float32
BF16 = jnp.bfloat16
I32 = jnp.int32

D_MODEL = 1024
DEPTH = 2
HEAD_DIM = 64
NEG_INF = -1e30
A_HEADS = 8
A_KV_HEADS = 2
A_REP = A_HEADS // A_KV_HEADS
A_WINDOW = 128
A_BLOCK = 128
A_Q = A_HEADS * HEAD_DIM
A_KV = A_KV_HEADS * HEAD_DIM
B_HEADS = 8
B_Q_RANK = 384
B_KV_RANK = 256
B_NOPE = 64
B_ROPE = 32
B_V = 64
B_PAD = 128
ROPE_BASE = 10000.0
C_HEADS = 16
GRID_W = 64
C_WIN_H = 8
C_WIN_W = 16
C_QROWS = 8
C_KROWS = 16
N_EXPERTS = 16
D_EXPERT = 2 * D_MODEL
TOK_CHUNK = 256
GATHER_CHUNK = 1024
COMBINE_TOKENS = 1024
LN_EPS = 1e-5
RMS_EPS = 1e-6
DN_ALPHA = (2 * DEPTH) ** 0.25

LANES = 128
VMEM_LIMIT = 56 * 1024 * 1024


def _cparams(sem):
    return pltpu.CompilerParams(dimension_semantics=sem, vmem_limit_bytes=VMEM_LIMIT)


def _layer_norm(z, g, b):
    mu = jnp.mean(z, -1, keepdims=True)
    zc = z - mu
    var = jnp.mean(zc * zc, -1, keepdims=True)
    return zc * lax.rsqrt(var + LN_EPS) * g + b


def _dot(a, b):
    return jnp.dot(a, b, preferred_element_type=F32)


def _dot_nt(a, b):
    return lax.dot_general(a, b, (((1,), (1,)), ((), ())), preferred_element_type=F32)


def _even_proj_kernel(x_ref, wa_ref, wcq_ref, wckv_ref, wkp_ref, wkr_ref, gq_ref, gkv_ref,
                      wq1_ref, wq2_ref, wkn_ref, wv_ref, cos_ref, sin_ref,
                      qkva_ref, q_ref, k_ref, v_ref, *, q_scale):
    xb = x_ref[...].astype(BF16)
    qkva_ref[...] = _dot(xb, wa_ref[...]).astype(BF16)
    cos = cos_ref[...]
    sin = sin_ref[...]

    cq = _dot(xb, wcq_ref[...])
    nq = (cq * lax.rsqrt(jnp.mean(cq * cq, -1, keepdims=True) + RMS_EPS) * gq_ref[...]).astype(BF16)
    qp = _dot(nq, wq1_ref[...])
    qr = _dot(nq, wq2_ref[...])
    for h in range(B_HEADS):
        sl = slice(h * B_PAD, (h + 1) * B_PAD)
        q_ref[:, sl] = ((qp[:, sl] * cos + qr[:, sl] * sin) * q_scale).astype(BF16)

    ckv = _dot(xb, wckv_ref[...])
    nkv = (ckv * lax.rsqrt(jnp.mean(ckv * ckv, -1, keepdims=True) + RMS_EPS) * gkv_ref[...]).astype(BF16)
    kn = _dot(nkv, wkn_ref[...])
    v_ref[...] = _dot(nkv, wv_ref[...]).astype(BF16)
    kpe = _dot(xb, wkp_ref[...]) * cos + _dot(xb, wkr_ref[...]) * sin
    for h in range(B_HEADS):
        sl = slice(h * B_PAD, (h + 1) * B_PAD)
        k_ref[:, sl] = (kn[:, sl] + kpe).astype(BF16)


def _even_proj(x2, w_in, gq, w_q_up, gkv, w_kv_up, seq):
    m = x2.shape[0]
    tm = 512
    o1, o2, o3 = A_Q, A_Q + A_KV, A_Q + 2 * A_KV
    o4, o5 = o3 + B_Q_RANK, o3 + B_Q_RANK + B_KV_RANK
    half = B_ROPE // 2
    wa = w_in[:, :o3].astype(BF16)
    wcq = w_in[:, o3:o4].astype(BF16)
    wckv = w_in[:, o4:o5].astype(BF16)
    wkr_raw = w_in[:, o5:]
    zpad = jnp.zeros((D_MODEL, B_PAD - B_NOPE - B_ROPE), F32)
    znope = jnp.zeros((D_MODEL, B_NOPE), F32)
    wkp = jnp.concatenate([znope, wkr_raw, zpad], 1).astype(BF16)
    wkr = jnp.concatenate([znope, -wkr_raw[:, half:], wkr_raw[:, :half], zpad], 1).astype(BF16)
    wq = w_q_up.reshape(B_Q_RANK, B_HEADS, B_NOPE + B_ROPE)
    wq_n, wq_r = wq[..., :B_NOPE], wq[..., B_NOPE:]
    zq = jnp.zeros((B_Q_RANK, B_HEADS, B_PAD - B_NOPE - B_ROPE), F32)
    wq1 = jnp.concatenate([wq_n, wq_r, zq], -1).reshape(B_Q_RANK, B_HEADS * B_PAD).astype(BF16)
    wq2 = jnp.concatenate([jnp.zeros_like(wq_n), -wq_r[..., half:], wq_r[..., :half], zq], -1)
    wq2 = wq2.reshape(B_Q_RANK, B_HEADS * B_PAD).astype(BF16)
    wkv = w_kv_up.reshape(B_KV_RANK, B_HEADS, B_NOPE + B_V)
    zk = jnp.zeros((B_KV_RANK, B_HEADS, B_PAD - B_NOPE), F32)
    wkn = jnp.concatenate([wkv[..., :B_NOPE], zk], -1).reshape(B_KV_RANK, B_HEADS * B_PAD).astype(BF16)
    wv = wkv[..., B_NOPE:].reshape(B_KV_RANK, B_HEADS * B_V).astype(BF16)
    inv = ROPE_BASE ** (-jnp.arange(half, dtype=F32) / half)
    inv_l = jnp.concatenate([jnp.zeros((B_NOPE,), F32), inv, inv, jnp.zeros((B_PAD - B_NOPE - B_ROPE,), F32)])
    ang = jnp.arange(seq, dtype=I32).astype(F32)[:, None] * inv_l[None, :]
    cos_t, sin_t = jnp.cos(ang), jnp.sin(ang)

    nseq = seq // tm
    full = lambda a: pl.BlockSpec(a.shape, lambda i: (0,) * a.ndim)
    row = lambda w: pl.BlockSpec((tm, w), lambda i: (i, 0))
    tab = pl.BlockSpec((tm, B_PAD), lambda i: (i % nseq, 0))
    gq2, gkv2 = gq.reshape(1, -1), gkv.reshape(1, -1)
    ins = [x2, wa, wcq, wckv, wkp, wkr, gq2, gkv2, wq1, wq2, wkn, wv, cos_t, sin_t]
    in_specs = [row(D_MODEL)] + [full(a) for a in ins[1:12]] + [tab, tab]
    out_shape = [jax.ShapeDtypeStruct((m, o3), BF16), jax.ShapeDtypeStruct((m, B_HEADS * B_PAD), BF16),
                 jax.ShapeDtypeStruct((m, B_HEADS * B_PAD), BF16), jax.ShapeDtypeStruct((m, B_HEADS * B_V), BF16)]
    out_specs = [row(o3), row(B_HEADS * B_PAD), row(B_HEADS * B_PAD), row(B_HEADS * B_V)]
    return pl.pallas_call(
        functools.partial(_even_proj_kernel, q_scale=(B_NOPE + B_ROPE) ** -0.5),
        grid=(m // tm,), in_specs=in_specs, out_specs=out_specs, out_shape=out_shape,
        compiler_params=_cparams(("parallel",)), name="even_proj")(*ins)


def _swa_kernel(q_ref, k_ref, v_ref, slope_ref, sink_ref, o_ref, *, seq):
    i = pl.program_id(2)
    span = A_BLOCK + 2 * A_WINDOW
    start = pl.multiple_of(i * A_BLOCK, A_BLOCK)
    q = q_ref[0, 0].reshape(A_REP * A_BLOCK, HEAD_DIM)
    kb = k_ref[0, 0, pl.ds(start, span), :]
    vb = v_ref[0, 0, pl.ds(start, span), :]
    s = _dot_nt(q, kb) * (HEAD_DIM ** -0.5)
    row = lax.broadcasted_iota(I32, (A_REP * A_BLOCK, span), 0)
    col = lax.broadcasted_iota(I32, (A_REP * A_BLOCK, span), 1)
    t = start + row % A_BLOCK
    src = start - A_WINDOW + col
    dist = jnp.abs(t - src)
    valid = (dist <= A_WINDOW) & (src >= 0) & (src < seq)
    s = s - slope_ref[0] * dist.astype(F32)
    s = jnp.where(valid, s, NEG_INF)
    sink = sink_ref[0]
    m = jnp.maximum(jnp.max(s, -1, keepdims=True), sink)
    p = jnp.exp(s - m)
    denom = jnp.sum(p, -1, keepdims=True) + jnp.exp(sink - m)
    o = _dot(p.astype(BF16), vb) / denom
    o_ref[0, 0] = o.reshape(A_REP, A_BLOCK, HEAD_DIM).astype(BF16)


def _swa(qa, ka, va, a_sink, bsz, seq):
    g, r = A_KV_HEADS, A_REP
    q5 = qa.reshape(bsz, seq, g, r, HEAD_DIM).transpose(0, 2, 3, 1, 4)
    pad = ((0, 0), (0, 0), (A_WINDOW, A_WINDOW), (0, 0))
    k4 = jnp.pad(ka.reshape(bsz, seq, g, HEAD_DIM).transpose(0, 2, 1, 3), pad)
    v4 = jnp.pad(va.reshape(bsz, seq, g, HEAD_DIM).transpose(0, 2, 1, 3), pad)
    slopes = 2.0 ** (-8.0 * (jnp.arange(A_HEADS, dtype=F32) + 1.0) / A_HEADS)
    slope_c = jnp.repeat(slopes.reshape(g, r), A_BLOCK, axis=1).reshape(g, r * A_BLOCK, 1)
    sink_c = jnp.repeat(a_sink.astype(F32).reshape(g, r), A_BLOCK, axis=1).reshape(g, r * A_BLOCK, 1)
    sp = seq + 2 * A_WINDOW
    out = pl.pallas_call(
        functools.partial(_swa_kernel, seq=seq),
        grid=(bsz, g, seq // A_BLOCK),
        in_specs=[pl.BlockSpec((1, 1, r, A_BLOCK, HEAD_DIM), lambda b, gg, i: (b, gg, 0, i, 0)),
                  pl.BlockSpec((1, 1, sp, HEAD_DIM), lambda b, gg, i: (b, gg, 0, 0)),
                  pl.BlockSpec((1, 1, sp, HEAD_DIM), lambda b, gg, i: (b, gg, 0, 0)),
                  pl.BlockSpec((1, r * A_BLOCK, 1), lambda b, gg, i: (gg, 0, 0)),
                  pl.BlockSpec((1, r * A_BLOCK, 1), lambda b, gg, i: (gg, 0, 0))],
        out_specs=pl.BlockSpec((1, 1, r, A_BLOCK, HEAD_DIM), lambda b, gg, i: (b, gg, 0, i, 0)),
        out_shape=jax.ShapeDtypeStruct((bsz, g, r, seq, HEAD_DIM), BF16),
        compiler_params=_cparams(("parallel", "parallel", "arbitrary")), name="swa_attn")(q5, k4, v4, slope_c, sink_c)
    return out.transpose(0, 3, 1, 2, 4).reshape(bsz, seq, A_Q)


def _mla_kernel(qt_ref, k_ref, vt_ref, o_ref, *, tk, n_chunks):
    tq = qt_ref.shape[-1]
    for h in range(B_HEADS):
        qh = qt_ref[0, h]

        def body(c, carry, h=h, qh=qh):
            m, l, acc = carry
            kc = k_ref[0, pl.ds(pl.multiple_of(c * tk, tk), tk), h * B_PAD:(h + 1) * B_PAD]
            s = _dot(kc, qh)
            m_new = jnp.maximum(m, jnp.max(s, 0, keepdims=True))
            alpha = jnp.exp(m - m_new)
            p = jnp.exp(s - m_new)
            l = l * alpha + jnp.sum(p, 0, keepdims=True)
            acc = acc * alpha + _dot(vt_ref[0, h, c], p.astype(BF16))
            return m_new, l, acc

        init = (jnp.full((1, tq), NEG_INF, F32), jnp.zeros((1, tq), F32), jnp.zeros((B_V, tq), F32))
        m, l, acc = lax.fori_loop(0, n_chunks, body, init)
        o_ref[0, h] = (acc / l).astype(BF16)


def _mla(qm, km, vm, bsz, seq):
    tq, tk = 512, 512
    nck = seq // tk
    qt = qm.reshape(bsz, seq, B_HEADS, B_PAD).transpose(0, 2, 3, 1)
    k3 = km.reshape(bsz, seq, B_HEADS * B_PAD)
    vt = vm.reshape(bsz, nck, tk, B_HEADS, B_V).transpose(0, 3, 1, 4, 2)
    out = pl.pallas_call(
        functools.partial(_mla_kernel, tk=tk, n_chunks=nck),
        grid=(bsz, seq // tq),
        in_specs=[pl.BlockSpec((1, B_HEADS, B_PAD, tq), lambda b, i: (b, 0, 0, i)),
                  pl.BlockSpec((1, seq, B_HEADS * B_PAD), lambda b, i: (b, 0, 0)),
                  pl.BlockSpec((1, B_HEADS, nck, B_V, tk), lambda b, i: (b, 0, 0, 0, 0))],
        out_specs=pl.BlockSpec((1, B_HEADS, B_V, tq), lambda b, i: (b, 0, 0, i)),
        out_shape=jax.ShapeDtypeStruct((bsz, B_HEADS, B_V, seq), BF16),
        compiler_params=_cparams(("parallel", "arbitrary")), name="mla_attn")(qt, k3, vt)
    return out.transpose(0, 3, 1, 2).reshape(bsz, seq, B_HEADS * B_V)


def _outproj_kernel(*refs, n_in):
    a_refs = refs[:n_in]
    w_refs = refs[n_in:2 * n_in]
    h_ref, g_ref, b_ref, wr_ref = refs[2 * n_in:2 * n_in + 4]
    hn_ref, hb_ref, lg_ref = refs[2 * n_in + 4:]
    mix = _dot(a_refs[0][...], w_refs[0][...])
    for a_ref, w_ref in zip(a_refs[1:], w_refs[1:]):
        mix = mix + _dot(a_ref[...], w_ref[...])
    hn = _layer_norm(DN_ALPHA * h_ref[...] + mix, g_ref[...], b_ref[...])
    hn_ref[...] = hn
    hb_ref[...] = hn.astype(BF16)
    lg_ref[0] = lax.dot_general(wr_ref[...], hn, (((1,), (1,)), ((), ())),
                                preferred_element_type=F32, precision=lax.Precision.HIGHEST)


def _outproj(a_list, w_list, h2, g, b, w_router, bsz, seq):
    m = h2.shape[0]
    tm = 512
    nseq = seq // tm
    n_in = len(a_list)
    w_list = [w.astype(BF16) for w in w_list]
    wr_t = w_router.T.astype(F32)
    row = lambda w: pl.BlockSpec((tm, w), lambda i: (i, 0))
    full = lambda a: pl.BlockSpec(a.shape, lambda i: (0,) * a.ndim)
    g2, b2 = g.reshape(1, -1).astype(F32), b.reshape(1, -1).astype(F32)
    ins = list(a_list) + w_list + [h2, g2, b2, wr_t]
    in_specs = ([row(a.shape[1]) for a in a_list] + [full(w) for w in w_list]
                + [row(D_MODEL), full(g2), full(b2), full(wr_t)])
    out_shape = [jax.ShapeDtypeStruct((m, D_MODEL), F32), jax.ShapeDtypeStruct((m, D_MODEL), BF16),
                 jax.ShapeDtypeStruct((bsz, N_EXPERTS, seq), F32)]
    out_specs = [row(D_MODEL), row(D_MODEL),
                 pl.BlockSpec((1, N_EXPERTS, tm), lambda i: (i // nseq, 0, i % nseq))]
    return pl.pallas_call(
        functools.partial(_outproj_kernel, n_in=n_in), grid=(m // tm,),
        in_specs=in_specs, out_specs=out_specs, out_shape=out_shape,
        compiler_params=_cparams(("parallel",)), name="outproj_ln")(*ins)


def _route_kernel(lg_ref, pos_ref, gate_ref, *, cap, seq):
    lg = lg_ref[0]
    mx = jnp.max(lg, 0, keepdims=True)
    ex = jnp.exp(lg - mx)
    aff = ex / jnp.sum(ex, 0, keepdims=True)
    gate_ref[0] = aff

    def bis(_, carry):
        lo, hi = carry
        mid = lo + ((hi - lo + 1) >> 1)
        cnt = jnp.sum((aff >= pltpu.bitcast(mid, F32)).astype(I32), 1, keepdims=True)
        ok = cnt >= cap
        return jnp.where(ok, mid, lo), jnp.where(ok, hi, mid - 1)

    lo0 = jnp.zeros((N_EXPERTS, 1), I32)
    hi0 = jnp.full((N_EXPERTS, 1), 0x7F800000, I32)
    thr, _ = lax.fori_loop(0, 32, bis, (lo0, hi0))
    gt = aff >= pltpu.bitcast(thr + 1, F32)
    eq = (aff >= pltpu.bitcast(thr, F32)) & jnp.logical_not(gt)
    need = cap - jnp.sum(gt.astype(I32), 1, keepdims=True)

    ck = TOK_CHUNK
    nck = seq // ck
    tri = (lax.broadcasted_iota(I32, (ck, ck), 0) <= lax.broadcasted_iota(I32, (ck, ck), 1)).astype(BF16)

    def chunk_cumsum(mask, k, run):
        mk = mask[:, k * ck:(k + 1) * ck]
        inc = _dot(mk.astype(BF16), tri).astype(I32)
        return mk, inc + run, run + inc[:, ck - 1:ck]

    run = jnp.zeros((N_EXPERTS, 1), I32)
    sel_chunks = []
    for k in range(nck):
        mk, inc, run = chunk_cumsum(eq, k, run)
        sel_chunks.append(gt[:, k * ck:(k + 1) * ck] | (mk & (inc <= need)))

    run = jnp.zeros((N_EXPERTS, 1), I32)
    for k in range(nck):
        mk = sel_chunks[k]
        inc = _dot(mk.astype(BF16), tri).astype(I32) + run
        pos_ref[0, :, k * ck:(k + 1) * ck] = jnp.where(mk, inc - 1, -1)
        run = inc[:, ck - 1:ck]


def _route(logits_t, cap):
    bsz, _, seq = logits_t.shape
    blk = pl.BlockSpec((1, N_EXPERTS, seq), lambda b: (b, 0, 0))
    return pl.pallas_call(
        functools.partial(_route_kernel, cap=cap, seq=seq), grid=(bsz,),
        in_specs=[blk], out_specs=[blk, blk],
        out_shape=[jax.ShapeDtypeStruct((bsz, N_EXPERTS, seq), I32),
                   jax.ShapeDtypeStruct((bsz, N_EXPERTS, seq), F32)],
        compiler_params=_cparams(("parallel",)), name="route_topk")(logits_t)


def _ffn_kernel(pos_ref, gate_ref, h_ref, wg_ref, wu_ref, wd_ref, y_ref, x_ref, gs_ref, *, cap, nck):
    x_ref[...] = jnp.zeros_like(x_ref)
    gs_ref[...] = jnp.zeros_like(gs_ref)
    slot = lax.broadcasted_iota(I32, (cap, GATHER_CHUNK), 0)

    def gather(k, _):
        match = slot == pos_ref[0, 0, pl.ds(k, 1), :]
        rows = pl.ds(pl.multiple_of(k * GATHER_CHUNK, GATHER_CHUNK), GATHER_CHUNK)
        x_ref[...] += _dot(match.astype(BF16), h_ref[0, rows, :])
        gs_ref[...] += jnp.sum(jnp.where(match, gate_ref[0, 0, pl.ds(k, 1), :], 0.0), 1, keepdims=True)
        return 0

    lax.fori_loop(0, nck, gather, 0)
    x = x_ref[...].astype(BF16)
    g = _dot(x, wg_ref[0])
    u = _dot(x, wu_ref[0])
    hid = (g * jax.nn.sigmoid(g) * u).astype(BF16)
    y_ref[0, 0] = (_dot(hid, wd_ref[0]) * gs_ref[...]).astype(BF16)


def _ffn(hb3, pos, gate, w_gate, w_up, w_down, cap):
    bsz, seq, d = hb3.shape
    ne, f = w_gate.shape[0], w_gate.shape[-1]
    nck = seq // GATHER_CHUNK
    pos4 = pos.reshape(bsz, ne, nck, GATHER_CHUNK)
    gate4 = gate.reshape(bsz, ne, nck, GATHER_CHUNK)
    tok = pl.BlockSpec((1, 1, nck, GATHER_CHUNK), lambda b, e: (b, e, 0, 0))
    return pl.pallas_call(
        functools.partial(_ffn_kernel, cap=cap, nck=nck), grid=(bsz, ne),
        in_specs=[tok, tok,
                  pl.BlockSpec((1, seq, d), lambda b, e: (b, 0, 0)),
                  pl.BlockSpec((1, d, f), lambda b, e: (e, 0, 0)),
                  pl.BlockSpec((1, d, f), lambda b, e: (e, 0, 0)),
                  pl.BlockSpec((1, f, d), lambda b, e: (e, 0, 0))],
        out_specs=pl.BlockSpec((1, 1, cap, d), lambda b, e: (b, e, 0, 0)),
        out_shape=jax.ShapeDtypeStruct((bsz, ne, cap, d), BF16),
        scratch_shapes=[pltpu.VMEM((cap, d), F32), pltpu.VMEM((cap, 1), F32)],
        compiler_params=_cparams(("parallel", "arbitrary")), name="moe_ffn")(
            pos4, gate4, hb3, w_gate.astype(BF16), w_up.astype(BF16), w_down.astype(BF16))


def _combine_kernel(pos_ref, y_ref, h_ref, g_ref, b_ref, hn_ref, hb_ref, *, cap, group):
    tt = h_ref.shape[1]
    ne = y_ref.shape[1]
    d = y_ref.shape[-1]
    slot = lax.broadcasted_iota(I32, (group, cap, tt), 1)
    acc = jnp.zeros((tt, d), F32)
    for e0 in range(0, ne, group):
        onehot = (slot == pos_ref[0, e0:e0 + group]).astype(BF16).reshape(group * cap, tt)
        ys = y_ref[0, e0:e0 + group].reshape(group * cap, d)
        acc = acc + lax.dot_general(onehot, ys, (((0,), (0,)), ((), ())), preferred_element_type=F32)
    hn = _layer_norm(DN_ALPHA * h_ref[0] + acc, g_ref[...], b_ref[...])
    hn_ref[0] = hn
    hb_ref[0] = hn.astype(BF16)


def _combine(y, pos, h3, g, b):
    bsz, ne, cap, d = y.shape
    seq = h3.shape[1]
    tt = COMBINE_TOKENS
    pos4 = pos.reshape(bsz, ne, 1, seq)
    g2, b2 = g.reshape(1, -1).astype(F32), b.reshape(1, -1).astype(F32)
    hblk = pl.BlockSpec((1, tt, d), lambda bb, k: (bb, k, 0))
    vec = pl.BlockSpec((1, d), lambda bb, k: (0, 0))
    return pl.pallas_call(
        functools.partial(_combine_kernel, cap=cap, group=4), grid=(bsz, seq // tt),
        in_specs=[pl.BlockSpec((1, ne, 1, tt), lambda bb, k: (bb, 0, 0, k)),
                  pl.BlockSpec((1, ne, cap, d), lambda bb, k: (bb, 0, 0, 0), pipeline_mode=pl.Buffered(1)),
                  hblk, vec, vec],
        out_specs=[hblk, hblk],
        out_shape=[jax.ShapeDtypeStruct(h3.shape, F32), jax.ShapeDtypeStruct(h3.shape, BF16)],
        compiler_params=_cparams(("parallel", "arbitrary")), name="moe_combine_ln")(pos4, y, h3, g2, b2)


def _moe_block(h2, hb2, logits_t, w_gate, w_up, w_down, g, b, bsz, seq):
    cap = 2 * seq // N_EXPERTS
    pos, gate = _route(logits_t, cap)
    y = _ffn(hb2.reshape(bsz, seq, D_MODEL), pos, gate, w_gate, w_up, w_down, cap)
    hn, hnb = _combine(y, pos, h2.reshape(bsz, seq, D_MODEL), g, b)
    return hn.reshape(bsz * seq, D_MODEL), hnb.reshape(bsz * seq, D_MODEL)


def _qkv_kernel(x_ref, w_ref, q_ref, k_ref, v_ref):
    x = x_ref[...]
    w = C_HEADS * HEAD_DIM
    q_ref[...] = (_dot(x, w_ref[:, :w]) * (HEAD_DIM ** -0.5)).astype(BF16)
    k_ref[...] = _dot(x, w_ref[:, w:2 * w]).astype(BF16)
    v_ref[...] = _dot(x, w_ref[:, 2 * w:]).astype(BF16)


def _qkv(hb2, w_qkv):
    m = hb2.shape[0]
    tm = 512
    w = C_HEADS * HEAD_DIM
    row = pl.BlockSpec((tm, w), lambda i: (i, 0))
    return pl.pallas_call(
        _qkv_kernel, grid=(m // tm,),
        in_specs=[pl.BlockSpec((tm, D_MODEL), lambda i: (i, 0)),
                  pl.BlockSpec((D_MODEL, 3 * w), lambda i: (0, 0))],
        out_specs=[row, row, row], out_shape=[jax.ShapeDtypeStruct((m, w), BF16)] * 3,
        compiler_params=_cparams(("parallel",)), name="qkv_proj")(hb2, w_qkv.astype(BF16))


def _na_kernel(q_ref, k_ref, v_ref, t2_ref, o_ref, s_ref, *, rows, hpg):
    rb = pl.program_id(2)
    nq = C_QROWS * GRID_W
    nk = C_KROWS * GRID_W
    r0 = rb * C_QROWS
    kr0 = jnp.clip(r0 - C_WIN_H // 2, 0, rows - C_KROWS)
    kstart = pl.multiple_of(kr0 * GRID_W, GRID_W)
    qi = lax.broadcasted_iota(I32, (nq, nk), 0)
    ki = lax.broadcasted_iota(I32, (nq, nk), 1)
    qrow, qcol = r0 + qi // GRID_W, qi % GRID_W
    krow, kcol = kr0 + ki // GRID_W, ki % GRID_W
    rs = jnp.clip(qrow - C_WIN_H // 2, 0, rows - C_WIN_H)
    cs = jnp.clip(qcol - C_WIN_W // 2, 0, GRID_W - C_WIN_W)
    valid = (krow >= rs) & (krow < rs + C_WIN_H) & (kcol >= cs) & (kcol < cs + C_WIN_W)
    mask_add = jnp.where(valid, 0.0, NEG_INF).astype(F32)
    dbase = kr0 - r0 + C_WIN_H - 1
    for h in range(hpg):
        s_ref[...] = _dot_nt(q_ref[0, h], k_ref[0, h, pl.ds(kstart, nk), :])
        vh = v_ref[0, h, pl.ds(kstart, nk), :]
        for j in range(C_QROWS):
            tiles = []
            for ip in range(C_KROWS // 2):
                d = jnp.clip(dbase + 2 * ip - j, -1, 2 * C_WIN_H - 2) + 1
                tiles.append(t2_ref[h, d])
            rsl = slice(j * GRID_W, (j + 1) * GRID_W)
            sj = s_ref[rsl, :] + jnp.concatenate(tiles, 1) + mask_add[rsl, :]
            m = jnp.max(sj, -1, keepdims=True)
            p = jnp.exp(sj - m)
            l = jnp.sum(p, -1, keepdims=True)
            o_ref[0, h, rsl, :] = (_dot(p.astype(BF16), vh) / l).astype(BF16)


def _natten(q2, k2, v2, rpb, bsz, seq):
    rows = seq // GRID_W
    hpg = 4
    heads = lambda a: a.reshape(bsz, seq, C_HEADS, HEAD_DIM).transpose(0, 2, 1, 3)
    q4, k4, v4 = heads(q2), heads(k2), heads(v2)
    cols = jnp.arange(GRID_W)
    dc = jnp.clip(cols[None, :] - cols[:, None] + C_WIN_W - 1, 0, 2 * C_WIN_W - 2)
    t = rpb.astype(F32)[:, :, dc]
    zero = jnp.zeros_like(t[:, :1])
    text = jnp.concatenate([zero, t, zero], 1)
    t2 = jnp.concatenate([text[:, :-1], text[:, 1:]], -1)
    nd = t2.shape[1]
    nq = C_QROWS * GRID_W
    out = pl.pallas_call(
        functools.partial(_na_kernel, rows=rows, hpg=hpg),
        grid=(bsz, C_HEADS // hpg, rows // C_QROWS),
        in_specs=[pl.BlockSpec((1, hpg, nq, HEAD_DIM), lambda b, g, r: (b, g, r, 0)),
                  pl.BlockSpec((1, hpg, seq, HEAD_DIM), lambda b, g, r: (b, g, 0, 0)),
                  pl.BlockSpec((1, hpg, seq, HEAD_DIM), lambda b, g, r: (b, g, 0, 0)),
                  pl.BlockSpec((hpg, nd, GRID_W, 2 * GRID_W), lambda b, g, r: (g, 0, 0, 0))],
        out_specs=pl.BlockSpec((1, hpg, nq, HEAD_DIM), lambda b, g, r: (b, g, r, 0)),
        out_shape=jax.ShapeDtypeStruct((bsz, C_HEADS, seq, HEAD_DIM), BF16),
        scratch_shapes=[pltpu.VMEM((nq, C_KROWS * GRID_W), F32)],
        compiler_params=_cparams(("parallel", "parallel", "arbitrary")), name="natten")(q4, k4, v4, t2)
    return out.transpose(0, 2, 1, 3).reshape(bsz * seq, C_HEADS * HEAD_DIM)


def kernel(x, w_in0, a_sink, mla_q_norm, w_q_up, mla_kv_norm, w_kv_up, w_out0, ln0a_g, ln0a_b,
           router0, w_gate0, w_up0, w_down0, ln0b_g, ln0b_b,
           w_qkv1, na_rpb, w_out1, ln1a_g, ln1a_b,
           router1, w_gate1, w_up1, w_down1, ln1b_g, ln1b_b):
    bsz, seq, d = x.shape
    x2 = x.reshape(bsz * seq, d)
    qkva, qm, km, vm = _even_proj(x2, w_in0, mla_q_norm, w_q_up, mla_kv_norm, w_kv_up, seq)
    qkva3 = qkva.reshape(bsz, seq, -1)
    out_a = _swa(qkva3[..., :A_Q], qkva3[..., A_Q:A_Q + A_KV], qkva3[..., A_Q + A_KV:], a_sink, bsz, seq)
    out_b = _mla(qm, km, vm, bsz, seq)
    h, hb, lg = _outproj([out_a.reshape(bsz * seq, A_Q), out_b.reshape(bsz * seq, -1)],
                         [w_out0[:A_Q], w_out0[A_Q:]], x2, ln0a_g, ln0a_b, router0, bsz, seq)
    h, hb = _moe_block(h, hb, lg, w_gate0, w_up0, w_down0, ln0b_g, ln0b_b, bsz, seq)
    q2, k2, v2 = _qkv(hb, w_qkv1)
    na = _natten(q2, k2, v2, na_rpb, bsz, seq)
    h, hb, lg = _outproj([na], [w_out1], h, ln1a_g, ln1a_b, router1, bsz, seq)
    h, _ = _moe_block(h, hb, lg, w_gate1, w_up1, w_down1, ln1b_g, ln1b_b, bsz, seq)
    return h.reshape(bsz, seq, d)
```

```python
import functools
import math

import jax
import jax.numpy as jnp
from jax import lax
from jax.experimental import pallas as pl
from jax.experimental.pallas import tpu as pltpu

F32 = jnp.float32
BF16 = jnp.bfloat16
I32 = jnp.int32

D_MODEL = 1024
DEPTH = 2
HEAD_DIM = 64
NEG_INF = -1e30
A_HEADS = 8
A_KV_HEADS = 2
A_REP = A_HEADS // A_KV_HEADS
A_WINDOW = 128
A_Q = A_HEADS * HEAD_DIM
A_KV = A_KV_HEADS * HEAD_DIM
A_TQ = 256
B_HEADS = 8
B_Q_RANK = 384
B_KV_RANK = 256
B_NOPE = 64
B_ROPE = 32
B_V = 64
B_PAD = 128
ROPE_BASE = 10000.0
C_HEADS = 16
GRID_W = 64
C_WIN_H = 8
C_WIN_W = 16
C_QROWS = 8
C_KROWS = 16
C_PAIRS = 4
N_EXPERTS = 16
D_EXPERT = 2 * D_MODEL
TOK_CHUNK = 256
GATHER_CHUNK = 1024
COMBINE_TOKENS = 1024
LN_EPS = 1e-5
RMS_EPS = 1e-6
DN_ALPHA = (2 * DEPTH) ** 0.25

LANES = 128
PAIR = 2 * HEAD_DIM
VMEM_LIMIT = 56 * 1024 * 1024
ROW_TILE = 512


def _cparams(sem):
    return pltpu.CompilerParams(dimension_semantics=sem, vmem_limit_bytes=VMEM_LIMIT)


def _layer_norm(z, g, b):
    mu = jnp.mean(z, -1, keepdims=True)
    zc = z - mu
    var = jnp.mean(zc * zc, -1, keepdims=True)
    return zc * lax.rsqrt(var + LN_EPS) * g + b


def _rms_norm(c, g):
    return c * lax.rsqrt(jnp.mean(c * c, -1, keepdims=True) + RMS_EPS) * g


def _dot(a, b):
    return jnp.dot(a, b, preferred_element_type=F32)


def _dot_nt(a, b):
    return lax.dot_general(a, b, (((1,), (1,)), ((), ())), preferred_element_type=F32)


def _dot_tn(a, b):
    return lax.dot_general(a, b, (((0,), (0,)), ((), ())), preferred_element_type=F32)


def _own_half(lo, hi):
    lane = lax.broadcasted_iota(I32, lo.shape, 1)
    return jnp.where(lane < HEAD_DIM, lo, hi)


def _even_proj_kernel(x_ref, wa_ref, wcq_ref, wckv_ref, wkp_ref, wkr_ref, gq_ref, gkv_ref,
                      wq1t_ref, wq2t_ref, wkn_ref, wvt_ref, cos_ref, sin_ref, cost_ref, sint_ref,
                      qa_ref, ka_ref, va_ref, qt_ref, k_ref, vt_ref, *, q_scale):
    xb = x_ref[...].astype(BF16)
    a = _dot(xb, wa_ref[...]).astype(BF16)
    qa_ref[...] = a[:, :A_Q]
    ka_ref[...] = a[:, A_Q:A_Q + 2 * A_KV]
    va_ref[...] = a[:, A_Q + 2 * A_KV:]

    nq = _rms_norm(_dot(xb, wcq_ref[...]), gq_ref[...]).astype(BF16)
    qpt = _dot_nt(wq1t_ref[...], nq)
    qrt = _dot_nt(wq2t_ref[...], nq)
    cost, sint = cost_ref[...], sint_ref[...]
    for h in range(B_HEADS):
        sl = slice(h * B_PAD, (h + 1) * B_PAD)
        qt_ref[0, h] = ((qpt[sl] * cost + qrt[sl] * sint) * q_scale).astype(BF16)

    nkv = _rms_norm(_dot(xb, wckv_ref[...]), gkv_ref[...]).astype(BF16)
    kn = _dot(nkv, wkn_ref[...])
    kpe = _dot(xb, wkp_ref[...]) * cos_ref[...] + _dot(xb, wkr_ref[...]) * sin_ref[...]
    for h in range(B_HEADS):
        sl = slice(h * B_PAD, (h + 1) * B_PAD)
        k_ref[:, sl] = (kn[:, sl] + kpe).astype(BF16)
    vt = _dot_nt(wvt_ref[...], nkv)
    vt_ref[0, :, 0] = vt.reshape(B_HEADS, B_V, vt.shape[-1]).astype(BF16)


def _even_proj(x2, w_in, gq, w_q_up, gkv, w_kv_up, bsz, seq):
    m = x2.shape[0]
    tm = ROW_TILE
    nseq = seq // tm
    o1, o2, o3 = A_Q, A_Q + A_KV, A_Q + 2 * A_KV
    o4, o5 = o3 + B_Q_RANK, o3 + B_Q_RANK + B_KV_RANK
    half = B_ROPE // 2
    wk = w_in[:, o1:o2].reshape(D_MODEL, A_KV_HEADS, 1, HEAD_DIM)
    wv = w_in[:, o2:o3].reshape(D_MODEL, A_KV_HEADS, 1, HEAD_DIM)
    dup = lambda w: jnp.broadcast_to(w, (D_MODEL, A_KV_HEADS, 2, HEAD_DIM)).reshape(D_MODEL, 2 * A_KV)
    wa = jnp.concatenate([w_in[:, :o1], dup(wk), dup(wv)], 1).astype(BF16)
    wcq = w_in[:, o3:o4].astype(BF16)
    wckv = w_in[:, o4:o5].astype(BF16)
    wkr_raw = w_in[:, o5:]
    zpad = jnp.zeros((D_MODEL, B_PAD - B_NOPE - B_ROPE), F32)
    znope = jnp.zeros((D_MODEL, B_NOPE), F32)
    wkp = jnp.concatenate([znope, wkr_raw, zpad], 1).astype(BF16)
    wkr = jnp.concatenate([znope, -wkr_raw[:, half:], wkr_raw[:, :half], zpad], 1).astype(BF16)
    wq = w_q_up.reshape(B_Q_RANK, B_HEADS, B_NOPE + B_ROPE)
    wq_n, wq_r = wq[..., :B_NOPE], wq[..., B_NOPE:]
    zq = jnp.zeros((B_Q_RANK, B_HEADS, B_PAD - B_NOPE - B_ROPE), F32)
    wq1t = jnp.concatenate([wq_n, wq_r, zq], -1).reshape(B_Q_RANK, B_HEADS * B_PAD).T.astype(BF16)
    wq2t = jnp.concatenate([jnp.zeros_like(wq_n), -wq_r[..., half:], wq_r[..., :half], zq], -1)
    wq2t = wq2t.reshape(B_Q_RANK, B_HEADS * B_PAD).T.astype(BF16)
    wkv = w_kv_up.reshape(B_KV_RANK, B_HEADS, B_NOPE + B_V)
    zk = jnp.zeros((B_KV_RANK, B_HEADS, B_PAD - B_NOPE), F32)
    wkn = jnp.concatenate([wkv[..., :B_NOPE], zk], -1).reshape(B_KV_RANK, B_HEADS * B_PAD).astype(BF16)
    wvt = wkv[..., B_NOPE:].reshape(B_KV_RANK, B_HEADS * B_V).T.astype(BF16)
    inv = ROPE_BASE ** (-jnp.arange(half, dtype=F32) / half)
    inv_l = jnp.concatenate([jnp.zeros((B_NOPE,), F32), inv, inv, jnp.zeros((B_PAD - B_NOPE - B_ROPE,), F32)])
    ang = jnp.arange(seq, dtype=I32).astype(F32)[:, None] * inv_l[None, :]
    cos_t, sin_t = jnp.cos(ang), jnp.sin(ang)

    full = lambda a: pl.BlockSpec(a.shape, lambda i: (0,) * a.ndim)
    row = lambda w: pl.BlockSpec((tm, w), lambda i: (i, 0))
    tab = pl.BlockSpec((tm, B_PAD), lambda i: (i % nseq, 0))
    tab_t = pl.BlockSpec((B_PAD, tm), lambda i: (0, i % nseq))
    gq2, gkv2 = gq.reshape(1, -1).astype(F32), gkv.reshape(1, -1).astype(F32)
    ins = [x2, wa, wcq, wckv, wkp, wkr, gq2, gkv2, wq1t, wq2t, wkn, wvt, cos_t, sin_t, cos_t.T, sin_t.T]
    in_specs = [row(D_MODEL)] + [full(a) for a in ins[1:12]] + [tab, tab, tab_t, tab_t]
    out_shape = [jax.ShapeDtypeStruct((m, A_Q), BF16),
                 jax.ShapeDtypeStruct((m, 2 * A_KV), BF16), jax.ShapeDtypeStruct((m, 2 * A_KV), BF16),
                 jax.ShapeDtypeStruct((bsz, B_HEADS, B_PAD, seq), BF16),
                 jax.ShapeDtypeStruct((m, B_HEADS * B_PAD), BF16),
                 jax.ShapeDtypeStruct((bsz, B_HEADS, nseq, B_V, tm), BF16)]
    out_specs = [row(A_Q), row(2 * A_KV), row(2 * A_KV),
                 pl.BlockSpec((1, B_HEADS, B_PAD, tm), lambda i: (i // nseq, 0, 0, i % nseq)),
                 row(B_HEADS * B_PAD),
                 pl.BlockSpec((1, B_HEADS, 1, B_V, tm), lambda i: (i // nseq, 0, i % nseq, 0, 0))]
    q_scale = (B_NOPE + B_ROPE) ** -0.5 * math.log2(math.e)
    return pl.pallas_call(
        functools.partial(_even_proj_kernel, q_scale=q_scale),
        grid=(m // tm,), in_specs=in_specs, out_specs=out_specs, out_shape=out_shape,
        compiler_params=_cparams(("parallel",)), name="even_proj")(*ins)


def _swa_kernel(sink_ref, q_ref, kp_ref, kc_ref, kn_ref, vp_ref, vc_ref, vn_ref, o_ref, *, seq):
    i = pl.program_id(1)
    tq = A_TQ
    span = tq + 2 * A_WINDOW
    start = i * tq
    kb = jnp.concatenate([kp_ref[0], kc_ref[0], kn_ref[0]], 0)
    vb = jnp.concatenate([vp_ref[0], vc_ref[0], vn_ref[0]], 0)
    t = start + lax.broadcasted_iota(I32, (tq, span), 0)
    src = start - A_WINDOW + lax.broadcasted_iota(I32, (tq, span), 1)
    dist = jnp.abs(t - src)
    valid = (dist <= A_WINDOW) & (src >= 0) & (src < seq)
    distf = dist.astype(F32)
    lane = lax.broadcasted_iota(I32, (tq, PAIR), 1)
    outs = []
    for h in range(A_HEADS):
        g = h // A_REP
        q_pair = q_ref[0, :, (h // 2) * PAIR:(h // 2 + 1) * PAIR]
        qm = jnp.where((lane >= HEAD_DIM) == (h % 2 == 1), q_pair, jnp.zeros_like(q_pair))
        s = _dot_nt(qm, kb[:, g * PAIR:(g + 1) * PAIR]) * (HEAD_DIM ** -0.5)
        s = s - (2.0 ** (-8.0 * (h + 1) / A_HEADS)) * distf
        s = jnp.where(valid, s, NEG_INF)
        sink = sink_ref[h]
        m = jnp.maximum(jnp.max(s, -1, keepdims=True), sink)
        p = jnp.exp(s - m)
        denom = jnp.sum(p, -1, keepdims=True) + jnp.exp(sink - m)
        outs.append(_dot(p.astype(BF16), vb[:, g * PAIR:(g + 1) * PAIR]) / denom)
    for jp in range(A_HEADS // 2):
        o_ref[0, :, jp * PAIR:(jp + 1) * PAIR] = _own_half(outs[2 * jp], outs[2 * jp + 1]).astype(BF16)


def _swa(qa3, ka3, va3, a_sink):
    bsz, seq, _ = qa3.shape
    tq, w = A_TQ, A_WINDOW
    nblk = seq // w
    r = tq // w
    kw = 2 * A_KV
    prev = pl.BlockSpec((1, w, kw), lambda b, i: (b, jnp.maximum(i * r - 1, 0), 0))
    cur = pl.BlockSpec((1, tq, kw), lambda b, i: (b, i, 0))
    nxt = pl.BlockSpec((1, w, kw), lambda b, i: (b, jnp.minimum(i * r + r, nblk - 1), 0))
    qblk = pl.BlockSpec((1, tq, A_Q), lambda b, i: (b, i, 0))
    return pl.pallas_call(
        functools.partial(_swa_kernel, seq=seq), grid=(bsz, seq // tq),
        in_specs=[pl.BlockSpec(memory_space=pltpu.SMEM), qblk, prev, cur, nxt, prev, cur, nxt],
        out_specs=qblk, out_shape=jax.ShapeDtypeStruct((bsz, seq, A_Q), BF16),
        compiler_params=_cparams(("parallel", "arbitrary")), name="swa_attn")(
            a_sink.astype(F32), qa3, ka3, ka3, ka3, va3, va3, va3)


def _mla_kernel(qt_ref, k_ref, vt_ref, o_ref, m_ref, l_ref, acc_ref, *, tk, n_chunks):
    m_ref[...] = jnp.full_like(m_ref, NEG_INF)
    l_ref[...] = jnp.zeros_like(l_ref)
    acc_ref[...] = jnp.zeros_like(acc_ref)

    def body(c, _):
        rows = pl.ds(pl.multiple_of(c * tk, tk), tk)
        for h in range(B_HEADS):
            s = _dot(k_ref[0, rows, h * B_PAD:(h + 1) * B_PAD], qt_ref[0, h])
            m_old = m_ref[h]
            m_new = jnp.maximum(m_old, jnp.max(s, 0, keepdims=True))
            alpha = jnp.exp2(m_old - m_new)
            p = jnp.exp2(s - m_new)
            l_ref[h] = l_ref[h] * alpha + jnp.sum(p, 0, keepdims=True)
            acc_ref[h] = acc_ref[h] * alpha + _dot(vt_ref[0, h, c], p.astype(BF16))
            m_ref[h] = m_new
        return 0

    lax.fori_loop(0, n_chunks, body, 0)
    for h in range(B_HEADS):
        o_ref[0, h] = (acc_ref[h] / l_ref[h]).astype(BF16)


def _mla(qt, k3, vt):
    bsz, _, _, seq = qt.shape
    tq = 512
    nck, tk = vt.shape[2], vt.shape[4]
    return pl.pallas_call(
        functools.partial(_mla_kernel, tk=tk, n_chunks=nck),
        grid=(bsz, seq // tq),
        in_specs=[pl.BlockSpec((1, B_HEADS, B_PAD, tq), lambda b, i: (b, 0, 0, i)),
                  pl.BlockSpec((1, seq, B_HEADS * B_PAD), lambda b, i: (b, 0, 0)),
                  pl.BlockSpec((1, B_HEADS, nck, B_V, tk), lambda b, i: (b, 0, 0, 0, 0))],
        out_specs=pl.BlockSpec((1, B_HEADS, B_V, tq), lambda b, i: (b, 0, 0, i)),
        out_shape=jax.ShapeDtypeStruct((bsz, B_HEADS, B_V, seq), BF16),
        scratch_shapes=[pltpu.VMEM((B_HEADS, 1, tq), F32), pltpu.VMEM((B_HEADS, 1, tq), F32),
                        pltpu.VMEM((B_HEADS, B_V, tq), F32)],
        compiler_params=_cparams(("parallel", "arbitrary")), name="mla_attn")(qt, k3, vt)


def _outproj_kernel(*refs, n_row, n_t):
    n_in = n_row + n_t
    a_refs = refs[:n_in]
    w_refs = refs[n_in:2 * n_in]
    h_ref, g_ref, b_ref, wr_ref = refs[2 * n_in:2 * n_in + 4]
    hn_ref, hb_ref, lg_ref = refs[2 * n_in + 4:]
    mix = None
    for a_ref, w_ref in zip(a_refs[:n_row], w_refs[:n_row]):
        part = _dot(a_ref[...], w_ref[...])
        mix = part if mix is None else mix + part
    for a_ref, w_ref in zip(a_refs[n_row:], w_refs[n_row:]):
        at = a_ref[0]
        part = _dot_tn(at.reshape(at.shape[0] * at.shape[1], at.shape[2]), w_ref[...])
        mix = part if mix is None else mix + part
    hn = _layer_norm(DN_ALPHA * h_ref[...] + mix, g_ref[...], b_ref[...])
    hn_ref[...] = hn
    hb_ref[...] = hn.astype(BF16)
    lg_ref[0] = lax.dot_general(wr_ref[...], hn, (((1,), (1,)), ((), ())),
                                preferred_element_type=F32, precision=lax.Precision.HIGHEST)


def _outproj(rows, rows_w, trans, trans_w, h2, g, b, w_router, bsz, seq):
    m = h2.shape[0]
    tm = ROW_TILE
    nseq = seq // tm
    w_list = [w.astype(BF16) for w in list(rows_w) + list(trans_w)]
    wr_t = w_router.T.astype(F32)
    row = lambda w: pl.BlockSpec((tm, w), lambda i: (i, 0))
    full = lambda a: pl.BlockSpec(a.shape, lambda i: (0,) * a.ndim)
    tblk = lambda a: pl.BlockSpec((1, a.shape[1], a.shape[2], tm), lambda i: (i // nseq, 0, 0, i % nseq))
    g2, b2 = g.reshape(1, -1).astype(F32), b.reshape(1, -1).astype(F32)
    ins = list(rows) + list(trans) + w_list + [h2, g2, b2, wr_t]
    in_specs = ([row(a.shape[1]) for a in rows] + [tblk(a) for a in trans] + [full(w) for w in w_list]
                + [row(D_MODEL), full(g2), full(b2), full(wr_t)])
    out_shape = [jax.ShapeDtypeStruct((m, D_MODEL), F32), jax.ShapeDtypeStruct((m, D_MODEL), BF16),
                 jax.ShapeDtypeStruct((bsz, N_EXPERTS, seq), F32)]
    out_specs = [row(D_MODEL), row(D_MODEL),
                 pl.BlockSpec((1, N_EXPERTS, tm), lambda i: (i // nseq, 0, i % nseq))]
    return pl.pallas_call(
        functools.partial(_outproj_kernel, n_row=len(rows), n_t=len(trans)), grid=(m // tm,),
        in_specs=in_specs, out_specs=out_specs, out_shape=out_shape,
        compiler_params=_cparams(("parallel",)), name="outproj_ln")(*ins)


def _route_kernel(lg_ref, pos_ref, gate_ref, *, cap, seq):
    lg = lg_ref[0]
    mx = jnp.max(lg, 0, keepdims=True)
    ex = jnp.exp(lg - mx)
    aff = ex / jnp.sum(ex, 0, keepdims=True)
    gate_ref[0] = aff

    def bis(_, carry):
        lo, hi = carry
        mid = lo + ((hi - lo + 1) >> 1)
        cnt = jnp.sum((aff >= pltpu.bitcast(mid, F32)).astype(I32), 1, keepdims=True)
        ok = cnt >= cap
        return jnp.where(ok, mid, lo), jnp.where(ok, hi, mid - 1)

    lo0 = jnp.zeros((N_EXPERTS, 1), I32)
    hi0 = jnp.full((N_EXPERTS, 1), 0x7F800000, I32)
    thr, _ = lax.fori_loop(0, 32, bis, (lo0, hi0))
    gt = aff >= pltpu.bitcast(thr + 1, F32)
    eq = (aff >= pltpu.bitcast(thr, F32)) & jnp.logical_not(gt)
    need = cap - jnp.sum(gt.astype(I32), 1, keepdims=True)

    ck = TOK_CHUNK
    nck = seq // ck
    tri = (lax.broadcasted_iota(I32, (ck, ck), 0) <= lax.broadcasted_iota(I32, (ck, ck), 1)).astype(BF16)

    def chunk_cumsum(mask, k, run):
        mk = mask[:, k * ck:(k + 1) * ck]
        inc = _dot(mk.astype(BF16), tri).astype(I32)
        return mk, inc + run, run + inc[:, ck - 1:ck]

    run = jnp.zeros((N_EXPERTS, 1), I32)
    sel_chunks = []
    for k in range(nck):
        mk, inc, run = chunk_cumsum(eq, k, run)
        sel_chunks.append(gt[:, k * ck:(k + 1) * ck] | (mk & (inc <= need)))

    run = jnp.zeros((N_EXPERTS, 1), I32)
    for k in range(nck):
        mk = sel_chunks[k]
        inc = _dot(mk.astype(BF16), tri).astype(I32) + run
        pos_ref[0, :, k * ck:(k + 1) * ck] = jnp.where(mk, inc - 1, -1)
        run = inc[:, ck - 1:ck]


def _route(logits_t, cap):
    bsz, _, seq = logits_t.shape
    blk = pl.BlockSpec((1, N_EXPERTS, seq), lambda b: (b, 0, 0))
    return pl.pallas_call(
        functools.partial(_route_kernel, cap=cap, seq=seq), grid=(bsz,),
        in_specs=[blk], out_specs=[blk, blk],
        out_shape=[jax.ShapeDtypeStruct((bsz, N_EXPERTS, seq), I32),
                   jax.ShapeDtypeStruct((bsz, N_EXPERTS, seq), F32)],
        compiler_params=_cparams(("parallel",)), name="route_topk")(logits_t)


def _ffn_kernel(pos_ref, gate_ref, h_ref, wg_ref, wu_ref, wd_ref, y_ref, x_ref, gs_ref, *, cap, nck):
    x_ref[...] = jnp.zeros_like(x_ref)
    gs_ref[...] = jnp.zeros_like(gs_ref)
    slot = lax.broadcasted_iota(I32, (cap, GATHER_CHUNK), 0)

    def gather(k, _):
        match = slot == pos_ref[0, 0, pl.ds(k, 1), :]
        rows = pl.ds(pl.multiple_of(k * GATHER_CHUNK, GATHER_CHUNK), GATHER_CHUNK)
        x_ref[...] += _dot(match.astype(BF16), h_ref[0, rows, :])
        gs_ref[...] += jnp.sum(jnp.where(match, gate_ref[0, 0, pl.ds(k, 1), :], 0.0), 1, keepdims=True)
        return 0

    lax.fori_loop(0, nck, gather, 0)
    x = x_ref[...].astype(BF16)
    g = _dot(x, wg_ref[0])
    u = _dot(x, wu_ref[0])
    hid = (g * jax.nn.sigmoid(g) * u).astype(BF16)
    y_ref[0, 0] = (_dot(hid, wd_ref[0]) * gs_ref[...]).astype(BF16)


def _ffn(hb3, pos, gate, w_gate, w_up, w_down, cap):
    bsz, seq, d = hb3.shape
    ne, f = w_gate.shape[0], w_gate.shape[-1]
    nck = seq // GATHER_CHUNK
    pos4 = pos.reshape(bsz, ne, nck, GATHER_CHUNK)
    gate4 = gate.reshape(bsz, ne, nck, GATHER_CHUNK)
    tok = pl.BlockSpec((1, 1, nck, GATHER_CHUNK), lambda b, e: (b, e, 0, 0))
    return pl.pallas_call(
        functools.partial(_ffn_kernel, cap=cap, nck=nck), grid=(bsz, ne),
        in_specs=[tok, tok,
                  pl.BlockSpec((1, seq, d), lambda b, e: (b, 0, 0)),
                  pl.BlockSpec((1, d, f), lambda b, e: (e, 0, 0)),
                  pl.BlockSpec((1, d, f), lambda b, e: (e, 0, 0)),
                  pl.BlockSpec((1, f, d), lambda b, e: (e, 0, 0))],
        out_specs=pl.BlockSpec((1, 1, cap, d), lambda b, e: (b, e, 0, 0)),
        out_shape=jax.ShapeDtypeStruct((bsz, ne, cap, d), BF16),
        scratch_shapes=[pltpu.VMEM((cap, d), F32), pltpu.VMEM((cap, 1), F32)],
        compiler_params=_cparams(("parallel", "arbitrary")), name="moe_ffn")(
            pos4, gate4, hb3, w_gate.astype(BF16), w_up.astype(BF16), w_down.astype(BF16))


def _combine_kernel(pos_ref, y_ref, h_ref, g_ref, b_ref, hn_ref, hb_ref, *, cap, group):
    tt = h_ref.shape[1]
    ne = y_ref.shape[1]
    d = y_ref.shape[-1]
    slot = lax.broadcasted_iota(I32, (group, cap, tt), 1)
    acc = jnp.zeros((tt, d), F32)
    for e0 in range(0, ne, group):
        onehot = (slot == pos_ref[0, e0:e0 + group]).astype(BF16).reshape(group * cap, tt)
        ys = y_ref[0, e0:e0 + group].reshape(group * cap, d)
        acc = acc + _dot_tn(onehot, ys)
    hn = _layer_norm(DN_ALPHA * h_ref[0] + acc, g_ref[...], b_ref[...])
    hn_ref[0] = hn
    hb_ref[0] = hn.astype(BF16)


def _combine(y, pos, h3, g, b):
    bsz, ne, cap, d = y.shape
    seq = h3.shape[1]
    tt = COMBINE_TOKENS
    pos4 = pos.reshape(bsz, ne, 1, seq)
    g2, b2 = g.reshape(1, -1).astype(F32), b.reshape(1, -1).astype(F32)
    hblk = pl.BlockSpec((1, tt, d), lambda bb, k: (bb, k, 0))
    vec = pl.BlockSpec((1, d), lambda bb, k: (0, 0))
    return pl.pallas_call(
        functools.partial(_combine_kernel, cap=cap, group=4), grid=(bsz, seq // tt),
        in_specs=[pl.BlockSpec((1, ne, 1, tt), lambda bb, k: (bb, 0, 0, k)),
                  pl.BlockSpec((1, ne, cap, d), lambda bb, k: (bb, 0, 0, 0), pipeline_mode=pl.Buffered(1)),
                  hblk, vec, vec],
        out_specs=[hblk, hblk],
        out_shape=[jax.ShapeDtypeStruct(h3.shape, F32), jax.ShapeDtypeStruct(h3.shape, BF16)],
        compiler_params=_cparams(("parallel", "arbitrary")), name="moe_combine_ln")(pos4, y, h3, g2, b2)


def _moe_block(h2, hb2, logits_t, w_gate, w_up, w_down, g, b, bsz, seq):
    cap = 2 * seq // N_EXPERTS
    pos, gate = _route(logits_t, cap)
    y = _ffn(hb2.reshape(bsz, seq, D_MODEL), pos, gate, w_gate, w_up, w_down, cap)
    hn, hnb = _combine(y, pos, h2.reshape(bsz, seq, D_MODEL), g, b)
    return hn.reshape(bsz * seq, D_MODEL), hnb.reshape(bsz * seq, D_MODEL)


def _qkv_kernel(x_ref, w_ref, q_ref, k_ref, v_ref):
    x = x_ref[...]
    w = C_HEADS * HEAD_DIM
    q_ref[...] = (_dot(x, w_ref[:, :w]) * (HEAD_DIM ** -0.5)).astype(BF16)
    k_ref[...] = _dot(x, w_ref[:, w:2 * w]).astype(BF16)
    v_ref[...] = _dot(x, w_ref[:, 2 * w:]).astype(BF16)


def _qkv(hb2, w_qkv):
    m = hb2.shape[0]
    tm = ROW_TILE
    w = C_HEADS * HEAD_DIM
    row = pl.BlockSpec((tm, w), lambda i: (i, 0))
    return pl.pallas_call(
        _qkv_kernel, grid=(m // tm,),
        in_specs=[pl.BlockSpec((tm, D_MODEL), lambda i: (i, 0)),
                  pl.BlockSpec((D_MODEL, 3 * w), lambda i: (0, 0))],
        out_specs=[row, row, row], out_shape=[jax.ShapeDtypeStruct((m, w), BF16)] * 3,
        compiler_params=_cparams(("parallel",)), name="qkv_proj")(hb2, w_qkv.astype(BF16))


def _na_kernel(q_ref, k_ref, v_ref, t2_ref, o_ref, s_ref, oacc_ref, *, rows):
    rb = pl.program_id(2)
    nq = C_QROWS * GRID_W
    nk = C_KROWS * GRID_W
    r0 = rb * C_QROWS
    kr0 = jnp.clip(r0 - C_WIN_H // 2, 0, rows - C_KROWS)
    kstart = pl.multiple_of(kr0 * GRID_W, GRID_W)
    qi = lax.broadcasted_iota(I32, (nq, nk), 0)
    ki = lax.broadcasted_iota(I32, (nq, nk), 1)
    qrow, qcol = r0 + qi // GRID_W, qi % GRID_W
    krow, kcol = kr0 + ki // GRID_W, ki % GRID_W
    rs = jnp.clip(qrow - C_WIN_H // 2, 0, rows - C_WIN_H)
    cs = jnp.clip(qcol - C_WIN_W // 2, 0, GRID_W - C_WIN_W)
    valid = (krow >= rs) & (krow < rs + C_WIN_H) & (kcol >= cs) & (kcol < cs + C_WIN_W)
    mask_add = jnp.where(valid, 0.0, NEG_INF).astype(F32)
    dbase = kr0 - r0 + C_WIN_H - 1
    lane = lax.broadcasted_iota(I32, (nq, PAIR), 1)
    for jp in range(C_PAIRS):
        psl = slice(jp * PAIR, (jp + 1) * PAIR)
        q_pair = q_ref[0, :, psl]
        k_pair = k_ref[0, pl.ds(kstart, nk), psl]
        v_pair = v_ref[0, pl.ds(kstart, nk), psl]
        for u in range(2):
            h = 2 * jp + u
            qm = jnp.where((lane >= HEAD_DIM) == (u == 1), q_pair, jnp.zeros_like(q_pair))
            s_ref[...] = _dot_nt(qm, k_pair)
            for j in range(C_QROWS):
                tiles = []
                for ip in range(C_KROWS // 2):
                    d = jnp.clip(dbase + 2 * ip - j, -1, 2 * C_WIN_H - 2) + 1
                    tiles.append(t2_ref[h, d])
                rsl = slice(j * GRID_W, (j + 1) * GRID_W)
                sj = s_ref[rsl, :] + jnp.concatenate(tiles, 1) + mask_add[rsl, :]
                m = jnp.max(sj, -1, keepdims=True)
                p = jnp.exp(sj - m)
                l = jnp.sum(p, -1, keepdims=True)
                oacc_ref[u, rsl, :] = _dot(p.astype(BF16), v_pair) / l
        o_ref[0, :, psl] = _own_half(oacc_ref[0], oacc_ref[1]).astype(BF16)


def _natten(q3, k3, v3, rpb):
    bsz, seq, _ = q3.shape
    rows = seq // GRID_W
    hps = 2 * C_PAIRS
    cols = jnp.arange(GRID_W)
    dc = jnp.clip(cols[None, :] - cols[:, None] + C_WIN_W - 1, 0, 2 * C_WIN_W - 2)
    t = rpb.astype(F32)[:, :, dc]
    zero = jnp.zeros_like(t[:, :1])
    text = jnp.concatenate([zero, t, zero], 1)
    t2 = jnp.concatenate([text[:, :-1], text[:, 1:]], -1)
    nd = t2.shape[1]
    nq = C_QROWS * GRID_W
    w = C_PAIRS * PAIR
    qblk = pl.BlockSpec((1, nq, w), lambda b, g, r: (b, r, g))
    kblk = pl.BlockSpec((1, seq, w), lambda b, g, r: (b, 0, g))
    return pl.pallas_call(
        functools.partial(_na_kernel, rows=rows),
        grid=(bsz, C_HEADS // hps, rows // C_QROWS),
        in_specs=[qblk, kblk, kblk,
                  pl.BlockSpec((hps, nd, GRID_W, 2 * GRID_W), lambda b, g, r: (g, 0, 0, 0))],
        out_specs=qblk, out_shape=jax.ShapeDtypeStruct((bsz, seq, C_HEADS * HEAD_DIM), BF16),
        scratch_shapes=[pltpu.VMEM((nq, C_KROWS * GRID_W), F32), pltpu.VMEM((2, nq, PAIR), F32)],
        compiler_params=_cparams(("parallel", "parallel", "arbitrary")), name="natten")(q3, k3, v3, t2)


def kernel(x, w_in0, a_sink, mla_q_norm, w_q_up, mla_kv_norm, w_kv_up, w_out0, ln0a_g, ln0a_b,
           router0, w_gate0, w_up0, w_down0, ln0b_g, ln0b_b,
           w_qkv1, na_rpb, w_out1, ln1a_g, ln1a_b,
           router1, w_gate1, w_up1, w_down1, ln1b_g, ln1b_b):
    bsz, seq, d = x.shape
    m = bsz * seq
    x2 = x.reshape(m, d)
    qa, ka, va, qt, km, vt = _even_proj(x2, w_in0, mla_q_norm, w_q_up, mla_kv_norm, w_kv_up, bsz, seq)
    out_a = _swa(qa.reshape(bsz, seq, -1), ka.reshape(bsz, seq, -1), va.reshape(bsz, seq, -1), a_sink)
    out_bt = _mla(qt, km.reshape(bsz, seq, -1), vt)
    h, hb, lg = _outproj([out_a.reshape(m, A_Q)], [w_out0[:A_Q]], [out_bt], [w_out0[A_Q:]],
                         x2, ln0a_g, ln0a_b, router0, bsz, seq)
    h, hb = _moe_block(h, hb, lg, w_gate0, w_up0, w_down0, ln0b_g, ln0b_b, bsz, seq)
    q2, k2, v2 = _qkv(hb, w_qkv1)
    r3 = lambda a: a.reshape(bsz, seq, -1)
    na = _natten(r3(q2), r3(k2), r3(v2), na_rpb)
    h, hb, lg = _outproj([na.reshape(m, -1)], [w_out1], [], [], h, ln1a_g, ln1a_b, router1, bsz, seq)
    h, _ = _moe_block(h, hb, lg, w_gate1, w_up1, w_down1, ln1b_g, ln1b_b, bsz, seq)
    return h.reshape(bsz, seq, d)
```

```python
import functools
import math

import jax
import jax.numpy as jnp
from jax import lax
from jax.experimental import pallas as pl
from jax.experimental.pallas import tpu as pltpu

F32 = jnp.float32
BF16 = jnp.bfloat16
I32 = jnp.int32

D_MODEL = 1024
DEPTH = 2
HEAD_DIM = 64
NEG_INF = -1e30
A_HEADS = 8
A_KV_HEADS = 2
A_REP = A_HEADS // A_KV_HEADS
A_WINDOW = 128
A_Q = A_HEADS * HEAD_DIM
A_KV = A_KV_HEADS * HEAD_DIM
A_TQ = 256
B_HEADS = 8
B_Q_RANK = 384
B_KV_RANK = 256
B_NOPE = 64
B_ROPE = 32
B_V = 64
B_PAD = 128
B_TQ = 256
ROPE_BASE = 10000.0
C_HEADS = 16
GRID_W = 64
C_WIN_H = 8
C_WIN_W = 16
C_QROWS = 4
C_KROWS = 12
C_PAIRS = 4
N_EXPERTS = 16
D_EXPERT = 2 * D_MODEL
TOK_CHUNK = 256
GATHER_CHUNK = 1024
COMBINE_TOKENS = 1024
FFN_CHUNK = 1024
LN_EPS = 1e-5
RMS_EPS = 1e-6
DN_ALPHA = (2 * DEPTH) ** 0.25

LANES = 128
PAIR = 2 * HEAD_DIM
VMEM_LIMIT = 56 * 1024 * 1024
ROW_TILE = 512


def _cparams(sem):
    return pltpu.CompilerParams(dimension_semantics=sem, vmem_limit_bytes=VMEM_LIMIT)


def _layer_norm(z, g, b):
    mu = jnp.mean(z, -1, keepdims=True)
    zc = z - mu
    var = jnp.mean(zc * zc, -1, keepdims=True)
    return zc * lax.rsqrt(var + LN_EPS) * g + b


def _rms_norm(c, g):
    return c * lax.rsqrt(jnp.mean(c * c, -1, keepdims=True) + RMS_EPS) * g


def _dot(a, b):
    return jnp.dot(a, b, preferred_element_type=F32)


def _dot_nt(a, b):
    return lax.dot_general(a, b, (((1,), (1,)), ((), ())), preferred_element_type=F32)


def _dot_tn(a, b):
    return lax.dot_general(a, b, (((0,), (0,)), ((), ())), preferred_element_type=F32)


def _own_half(lo, hi):
    lane = lax.broadcasted_iota(I32, lo.shape, 1)
    return jnp.where(lane < HEAD_DIM, lo, hi)


def _even_proj_kernel(x_ref, wa_ref, wcq_ref, wckv_ref, wkp_ref, wkr_ref, gq_ref, gkv_ref,
                      wq1t_ref, wq2t_ref, wkn_ref, wvt_ref, cos_ref, sin_ref, cost_ref, sint_ref,
                      qa_ref, ka_ref, va_ref, qt_ref, k_ref, vt_ref, *, q_scale):
    xb = x_ref[...].astype(BF16)
    a = _dot(xb, wa_ref[...]).astype(BF16)
    qa_ref[...] = a[:, :A_Q]
    ka_ref[...] = a[:, A_Q:A_Q + 2 * A_KV]
    va_ref[...] = a[:, A_Q + 2 * A_KV:]

    nq = _rms_norm(_dot(xb, wcq_ref[...]), gq_ref[...]).astype(BF16)
    qpt = _dot_nt(wq1t_ref[...], nq)
    qrt = _dot_nt(wq2t_ref[...], nq)
    cost, sint = cost_ref[...], sint_ref[...]
    for h in range(B_HEADS):
        sl = slice(h * B_PAD, (h + 1) * B_PAD)
        qt_ref[0, h] = ((qpt[sl] * cost + qrt[sl] * sint) * q_scale).astype(BF16)

    nkv = _rms_norm(_dot(xb, wckv_ref[...]), gkv_ref[...]).astype(BF16)
    kn = _dot(nkv, wkn_ref[...])
    kpe = _dot(xb, wkp_ref[...]) * cos_ref[...] + _dot(xb, wkr_ref[...]) * sin_ref[...]
    for h in range(B_HEADS):
        sl = slice(h * B_PAD, (h + 1) * B_PAD)
        k_ref[:, sl] = (kn[:, sl] + kpe).astype(BF16)
    vt = _dot_nt(wvt_ref[...], nkv)
    vt_ref[0, :, 0] = vt.reshape(B_HEADS, B_V, vt.shape[-1]).astype(BF16)


def _even_proj(x2, w_in, gq, w_q_up, gkv, w_kv_up, bsz, seq):
    m = x2.shape[0]
    tm = ROW_TILE
    nseq = seq // tm
    o1, o2, o3 = A_Q, A_Q + A_KV, A_Q + 2 * A_KV
    o4, o5 = o3 + B_Q_RANK, o3 + B_Q_RANK + B_KV_RANK
    half = B_ROPE // 2
    wk = w_in[:, o1:o2].reshape(D_MODEL, A_KV_HEADS, 1, HEAD_DIM)
    wv = w_in[:, o2:o3].reshape(D_MODEL, A_KV_HEADS, 1, HEAD_DIM)
    dup = lambda w: jnp.broadcast_to(w, (D_MODEL, A_KV_HEADS, 2, HEAD_DIM)).reshape(D_MODEL, 2 * A_KV)
    wa = jnp.concatenate([w_in[:, :o1], dup(wk), dup(wv)], 1).astype(BF16)
    wcq = w_in[:, o3:o4].astype(BF16)
    wckv = w_in[:, o4:o5].astype(BF16)
    wkr_raw = w_in[:, o5:]
    zpad = jnp.zeros((D_MODEL, B_PAD - B_NOPE - B_ROPE), F32)
    znope = jnp.zeros((D_MODEL, B_NOPE), F32)
    wkp = jnp.concatenate([znope, wkr_raw, zpad], 1).astype(BF16)
    wkr = jnp.concatenate([znope, -wkr_raw[:, half:], wkr_raw[:, :half], zpad], 1).astype(BF16)
    wq = w_q_up.reshape(B_Q_RANK, B_HEADS, B_NOPE + B_ROPE)
    wq_n, wq_r = wq[..., :B_NOPE], wq[..., B_NOPE:]
    zq = jnp.zeros((B_Q_RANK, B_HEADS, B_PAD - B_NOPE - B_ROPE), F32)
    wq1t = jnp.concatenate([wq_n, wq_r, zq], -1).reshape(B_Q_RANK, B_HEADS * B_PAD).T.astype(BF16)
    wq2t = jnp.concatenate([jnp.zeros_like(wq_n), -wq_r[..., half:], wq_r[..., :half], zq], -1)
    wq2t = wq2t.reshape(B_Q_RANK, B_HEADS * B_PAD).T.astype(BF16)
    wkv = w_kv_up.reshape(B_KV_RANK, B_HEADS, B_NOPE + B_V)
    zk = jnp.zeros((B_KV_RANK, B_HEADS, B_PAD - B_NOPE), F32)
    wkn = jnp.concatenate([wkv[..., :B_NOPE], zk], -1).reshape(B_KV_RANK, B_HEADS * B_PAD).astype(BF16)
    wvt = wkv[..., B_NOPE:].reshape(B_KV_RANK, B_HEADS * B_V).T.astype(BF16)
    inv = ROPE_BASE ** (-jnp.arange(half, dtype=F32) / half)
    inv_l = jnp.concatenate([jnp.zeros((B_NOPE,), F32), inv, inv, jnp.zeros((B_PAD - B_NOPE - B_ROPE,), F32)])
    ang = jnp.arange(seq, dtype=I32).astype(F32)[:, None] * inv_l[None, :]
    cos_t, sin_t = jnp.cos(ang), jnp.sin(ang)

    full = lambda a: pl.BlockSpec(a.shape, lambda i: (0,) * a.ndim)
    row = lambda w: pl.BlockSpec((tm, w), lambda i: (i, 0))
    tab = pl.BlockSpec((tm, B_PAD), lambda i: (i % nseq, 0))
    tab_t = pl.BlockSpec((B_PAD, tm), lambda i: (0, i % nseq))
    gq2, gkv2 = gq.reshape(1, -1).astype(F32), gkv.reshape(1, -1).astype(F32)
    ins = [x2, wa, wcq, wckv, wkp, wkr, gq2, gkv2, wq1t, wq2t, wkn, wvt, cos_t, sin_t, cos_t.T, sin_t.T]
    in_specs = [row(D_MODEL)] + [full(a) for a in ins[1:12]] + [tab, tab, tab_t, tab_t]
    out_shape = [jax.ShapeDtypeStruct((m, A_Q), BF16),
                 jax.ShapeDtypeStruct((m, 2 * A_KV), BF16), jax.ShapeDtypeStruct((m, 2 * A_KV), BF16),
                 jax.ShapeDtypeStruct((bsz, B_HEADS, B_PAD, seq), BF16),
                 jax.ShapeDtypeStruct((m, B_HEADS * B_PAD), BF16),
                 jax.ShapeDtypeStruct((bsz, B_HEADS, nseq, B_V, tm), BF16)]
    out_specs = [row(A_Q), row(2 * A_KV), row(2 * A_KV),
                 pl.BlockSpec((1, B_HEADS, B_PAD, tm), lambda i: (i // nseq, 0, 0, i % nseq)),
                 row(B_HEADS * B_PAD),
                 pl.BlockSpec((1, B_HEADS, 1, B_V, tm), lambda i: (i // nseq, 0, i % nseq, 0, 0))]
    q_scale = (B_NOPE + B_ROPE) ** -0.5 * math.log2(math.e)
    return pl.pallas_call(
        functools.partial(_even_proj_kernel, q_scale=q_scale),
        grid=(m // tm,), in_specs=in_specs, out_specs=out_specs, out_shape=out_shape,
        compiler_params=_cparams(("parallel",)), name="even_proj")(*ins)


def _swa_kernel(sink_ref, q_ref, kp_ref, kc_ref, kn_ref, vp_ref, vc_ref, vn_ref, o_ref, *, seq):
    i = pl.program_id(1)
    tq = A_TQ
    span = tq + 2 * A_WINDOW
    start = i * tq
    kb = jnp.concatenate([kp_ref[0], kc_ref[0], kn_ref[0]], 0)
    vb = jnp.concatenate([vp_ref[0], vc_ref[0], vn_ref[0]], 0)
    t = start + lax.broadcasted_iota(I32, (tq, span), 0)
    src = start - A_WINDOW + lax.broadcasted_iota(I32, (tq, span), 1)
    dist = jnp.abs(t - src)
    valid = (dist <= A_WINDOW) & (src >= 0) & (src < seq)
    distf = dist.astype(F32)
    lane = lax.broadcasted_iota(I32, (tq, PAIR), 1)
    outs = []
    for h in range(A_HEADS):
        g = h // A_REP
        q_pair = q_ref[0, :, (h // 2) * PAIR:(h // 2 + 1) * PAIR]
        qm = jnp.where((lane >= HEAD_DIM) == (h % 2 == 1), q_pair, jnp.zeros_like(q_pair))
        s = _dot_nt(qm, kb[:, g * PAIR:(g + 1) * PAIR]) * (HEAD_DIM ** -0.5)
        s = s - (2.0 ** (-8.0 * (h + 1) / A_HEADS)) * distf
        s = jnp.where(valid, s, NEG_INF)
        sink = sink_ref[h]
        m = jnp.maximum(jnp.max(s, -1, keepdims=True), sink)
        p = jnp.exp(s - m)
        denom = jnp.sum(p, -1, keepdims=True) + jnp.exp(sink - m)
        outs.append(_dot(p.astype(BF16), vb[:, g * PAIR:(g + 1) * PAIR]) / denom)
    for jp in range(A_HEADS // 2):
        o_ref[0, :, jp * PAIR:(jp + 1) * PAIR] = _own_half(outs[2 * jp], outs[2 * jp + 1]).astype(BF16)


def _swa(qa3, ka3, va3, a_sink):
    bsz, seq, _ = qa3.shape
    tq, w = A_TQ, A_WINDOW
    nblk = seq // w
    r = tq // w
    kw = 2 * A_KV
    prev = pl.BlockSpec((1, w, kw), lambda b, i: (b, jnp.maximum(i * r - 1, 0), 0))
    cur = pl.BlockSpec((1, tq, kw), lambda b, i: (b, i, 0))
    nxt = pl.BlockSpec((1, w, kw), lambda b, i: (b, jnp.minimum(i * r + r, nblk - 1), 0))
    qblk = pl.BlockSpec((1, tq, A_Q), lambda b, i: (b, i, 0))
    return pl.pallas_call(
        functools.partial(_swa_kernel, seq=seq), grid=(bsz, seq // tq),
        in_specs=[pl.BlockSpec(memory_space=pltpu.SMEM), qblk, prev, cur, nxt, prev, cur, nxt],
        out_specs=qblk, out_shape=jax.ShapeDtypeStruct((bsz, seq, A_Q), BF16),
        compiler_params=_cparams(("parallel", "arbitrary")), name="swa_attn")(
            a_sink.astype(F32), qa3, ka3, ka3, ka3, va3, va3, va3)


def _mla_kernel(qt_ref, k_ref, vt_ref, o_ref, s_ref, cm_ref, m_ref, l_ref, acc_ref, *, tk, n_chunks):
    m_ref[...] = jnp.full_like(m_ref, NEG_INF)
    l_ref[...] = jnp.zeros_like(l_ref)
    acc_ref[...] = jnp.zeros_like(acc_ref)

    def scores(c, slot):
        rows = pl.ds(pl.multiple_of(c * tk, tk), tk)
        for h in range(B_HEADS):
            s = _dot(k_ref[0, rows, h * B_PAD:(h + 1) * B_PAD], qt_ref[0, h])
            s_ref[slot, h] = s
            cm_ref[slot, h] = jnp.max(s, 0, keepdims=True)

    def update(c, slot):
        for h in range(B_HEADS):
            m_old = m_ref[h]
            m_new = jnp.maximum(m_old, cm_ref[slot, h])
            alpha = jnp.exp2(m_old - m_new)
            p = jnp.exp2(s_ref[slot, h] - m_new)
            l_ref[h] = l_ref[h] * alpha + jnp.sum(p, 0, keepdims=True)
            acc_ref[h] = acc_ref[h] * alpha + _dot(vt_ref[0, h, c], p.astype(BF16))
            m_ref[h] = m_new

    scores(0, 0)

    def body(c, _):
        slot = lax.rem(c, 2)
        scores(c + 1, 1 - slot)
        update(c, slot)
        return 0

    lax.fori_loop(0, n_chunks - 1, body, 0)
    update(n_chunks - 1, (n_chunks - 1) % 2)
    for h in range(B_HEADS):
        o_ref[0, h] = (acc_ref[h] / l_ref[h]).astype(BF16)


def _mla(qt, k3, vt):
    bsz, _, _, seq = qt.shape
    tq = B_TQ
    nck, tk = vt.shape[2], vt.shape[4]
    return pl.pallas_call(
        functools.partial(_mla_kernel, tk=tk, n_chunks=nck),
        grid=(bsz, seq // tq),
        in_specs=[pl.BlockSpec((1, B_HEADS, B_PAD, tq), lambda b, i: (b, 0, 0, i)),
                  pl.BlockSpec((1, seq, B_HEADS * B_PAD), lambda b, i: (b, 0, 0)),
                  pl.BlockSpec((1, B_HEADS, nck, B_V, tk), lambda b, i: (b, 0, 0, 0, 0))],
        out_specs=pl.BlockSpec((1, B_HEADS, B_V, tq), lambda b, i: (b, 0, 0, i)),
        out_shape=jax.ShapeDtypeStruct((bsz, B_HEADS, B_V, seq), BF16),
        scratch_shapes=[pltpu.VMEM((2, B_HEADS, tk, tq), F32), pltpu.VMEM((2, B_HEADS, 1, tq), F32),
                        pltpu.VMEM((B_HEADS, 1, tq), F32), pltpu.VMEM((B_HEADS, 1, tq), F32),
                        pltpu.VMEM((B_HEADS, B_V, tq), F32)],
        compiler_params=_cparams(("parallel", "arbitrary")), name="mla_attn")(qt, k3, vt)


def _outproj_kernel(*refs, n_row, n_t):
    n_in = n_row + n_t
    a_refs = refs[:n_in]
    w_refs = refs[n_in:2 * n_in]
    h_ref, g_ref, b_ref, wr_ref = refs[2 * n_in:2 * n_in + 4]
    hn_ref, hb_ref, lg_ref = refs[2 * n_in + 4:]
    mix = None
    for a_ref, w_ref in zip(a_refs[:n_row], w_refs[:n_row]):
        part = _dot(a_ref[...], w_ref[...])
        mix = part if mix is None else mix + part
    for a_ref, w_ref in zip(a_refs[n_row:], w_refs[n_row:]):
        at = a_ref[0]
        part = _dot_tn(at.reshape(at.shape[0] * at.shape[1], at.shape[2]), w_ref[...])
        mix = part if mix is None else mix + part
    hn = _layer_norm(DN_ALPHA * h_ref[...] + mix, g_ref[...], b_ref[...])
    hn_ref[...] = hn
    hb_ref[...] = hn.astype(BF16)
    lg = jnp.dot(hn, wr_ref[...], preferred_element_type=F32, precision=lax.Precision.HIGHEST)
    lg_ref[0] = lg.T[:N_EXPERTS]


def _outproj(rows, rows_w, trans, trans_w, h2, g, b, w_router, bsz, seq):
    m = h2.shape[0]
    tm = ROW_TILE
    nseq = seq // tm
    w_list = [w.astype(BF16) for w in list(rows_w) + list(trans_w)]
    wr_t = jnp.pad(w_router.astype(F32), ((0, 0), (0, LANES - N_EXPERTS)))
    row = lambda w: pl.BlockSpec((tm, w), lambda i: (i, 0))
    full = lambda a: pl.BlockSpec(a.shape, lambda i: (0,) * a.ndim)
    tblk = lambda a: pl.BlockSpec((1, a.shape[1], a.shape[2], tm), lambda i: (i // nseq, 0, 0, i % nseq))
    g2, b2 = g.reshape(1, -1).astype(F32), b.reshape(1, -1).astype(F32)
    ins = list(rows) + list(trans) + w_list + [h2, g2, b2, wr_t]
    in_specs = ([row(a.shape[1]) for a in rows] + [tblk(a) for a in trans] + [full(w) for w in w_list]
                + [row(D_MODEL), full(g2), full(b2), full(wr_t)])
    out_shape = [jax.ShapeDtypeStruct((m, D_MODEL), F32), jax.ShapeDtypeStruct((m, D_MODEL), BF16),
                 jax.ShapeDtypeStruct((bsz, N_EXPERTS, seq), F32)]
    out_specs = [row(D_MODEL), row(D_MODEL),
                 pl.BlockSpec((1, N_EXPERTS, tm), lambda i: (i // nseq, 0, i % nseq))]
    return pl.pallas_call(
        functools.partial(_outproj_kernel, n_row=len(rows), n_t=len(trans)), grid=(m // tm,),
        in_specs=in_specs, out_specs=out_specs, out_shape=out_shape,
        compiler_params=_cparams(("parallel",)), name="outproj_ln")(*ins)


def _route_kernel(lg_ref, pos_ref, gate_ref, *, cap, seq):
    lg = lg_ref[0]
    mx = jnp.max(lg, 0, keepdims=True)
    ex = jnp.exp(lg - mx)
    aff = ex / jnp.sum(ex, 0, keepdims=True)
    gate_ref[0] = aff

    def bis(_, carry):
        lo, hi = carry
        mid = lo + ((hi - lo + 1) >> 1)
        cnt = jnp.sum((aff >= pltpu.bitcast(mid, F32)).astype(I32), 1, keepdims=True)
        ok = cnt >= cap
        return jnp.where(ok, mid, lo), jnp.where(ok, hi, mid - 1)

    lo0 = jnp.zeros((N_EXPERTS, 1), I32)
    hi0 = jnp.full((N_EXPERTS, 1), 0x7F800000, I32)
    thr, _ = lax.fori_loop(0, 32, bis, (lo0, hi0))
    gt = aff >= pltpu.bitcast(thr + 1, F32)
    eq = (aff >= pltpu.bitcast(thr, F32)) & jnp.logical_not(gt)
    need = cap - jnp.sum(gt.astype(I32), 1, keepdims=True)

    ck = TOK_CHUNK
    nck = seq // ck
    tri = (lax.broadcasted_iota(I32, (ck, ck), 0) <= lax.broadcasted_iota(I32, (ck, ck), 1)).astype(BF16)

    def chunk_cumsum(mask, k, run):
        mk = mask[:, k * ck:(k + 1) * ck]
        inc = _dot(mk.astype(BF16), tri).astype(I32)
        return mk, inc + run, run + inc[:, ck - 1:ck]

    run = jnp.zeros((N_EXPERTS, 1), I32)
    sel_chunks = []
    for k in range(nck):
        mk, inc, run = chunk_cumsum(eq, k, run)
        sel_chunks.append(gt[:, k * ck:(k + 1) * ck] | (mk & (inc <= need)))

    run = jnp.zeros((N_EXPERTS, 1), I32)
    for k in range(nck):
        mk = sel_chunks[k]
        inc = _dot(mk.astype(BF16), tri).astype(I32) + run
        pos_ref[0, :, k * ck:(k + 1) * ck] = jnp.where(mk, inc - 1, -1)
        run = inc[:, ck - 1:ck]


def _route(logits_t, cap):
    bsz, _, seq = logits_t.shape
    blk = pl.BlockSpec((1, N_EXPERTS, seq), lambda b: (b, 0, 0))
    return pl.pallas_call(
        functools.partial(_route_kernel, cap=cap, seq=seq), grid=(bsz,),
        in_specs=[blk], out_specs=[blk, blk],
        out_shape=[jax.ShapeDtypeStruct((bsz, N_EXPERTS, seq), I32),
                   jax.ShapeDtypeStruct((bsz, N_EXPERTS, seq), F32)],
        compiler_params=_cparams(("parallel",)), name="route_topk")(logits_t)


def _ffn_kernel(pos_ref, gate_ref, h_ref, wg_ref, wu_ref, wd_ref, y_ref, x_ref, gs_ref, acc_ref, *, cap, nck):
    fc = pl.program_id(2)

    @pl.when(fc == 0)
    def _():
        acc_ref[...] = jnp.zeros_like(acc_ref)
        gs_ref[...] = jnp.zeros_like(gs_ref)
        slot = lax.broadcasted_iota(I32, (cap, GATHER_CHUNK), 0)

        def gather(k, _):
            match = slot == pos_ref[0, 0, pl.ds(k, 1), :]
            rows = pl.ds(pl.multiple_of(k * GATHER_CHUNK, GATHER_CHUNK), GATHER_CHUNK)
            acc_ref[...] += _dot(match.astype(BF16), h_ref[0, rows, :])
            gs_ref[...] += jnp.sum(jnp.where(match, gate_ref[0, 0, pl.ds(k, 1), :], 0.0), 1, keepdims=True)
            return 0

        lax.fori_loop(0, nck, gather, 0)
        x_ref[...] = acc_ref[...].astype(BF16)
        acc_ref[...] = jnp.zeros_like(acc_ref)

    x = x_ref[...]
    g = _dot(x, wg_ref[0].astype(BF16))
    u = _dot(x, wu_ref[0].astype(BF16))
    hid = (g * jax.nn.sigmoid(g) * u).astype(BF16)
    acc_ref[...] += _dot(hid, wd_ref[0].astype(BF16))

    @pl.when(fc == pl.num_programs(2) - 1)
    def _():
        y_ref[0, 0] = (acc_ref[...] * gs_ref[...]).astype(BF16)


def _ffn(hb3, pos, gate, w_gate, w_up, w_down, cap):
    bsz, seq, d = hb3.shape
    ne, f = w_gate.shape[0], w_gate.shape[-1]
    nck = seq // GATHER_CHUNK
    fck = FFN_CHUNK
    pos4 = pos.reshape(bsz, ne, nck, GATHER_CHUNK)
    gate4 = gate.reshape(bsz, ne, nck, GATHER_CHUNK)
    tok = pl.BlockSpec((1, 1, nck, GATHER_CHUNK), lambda b, e, c: (b, e, 0, 0))
    return pl.pallas_call(
        functools.partial(_ffn_kernel, cap=cap, nck=nck), grid=(bsz, ne, f // fck),
        in_specs=[tok, tok,
                  pl.BlockSpec((1, seq, d), lambda b, e, c: (b, 0, 0)),
                  pl.BlockSpec((1, d, fck), lambda b, e, c: (e, 0, c)),
                  pl.BlockSpec((1, d, fck), lambda b, e, c: (e, 0, c)),
                  pl.BlockSpec((1, fck, d), lambda b, e, c: (e, c, 0))],
        out_specs=pl.BlockSpec((1, 1, cap, d), lambda b, e, c: (b, e, 0, 0)),
        out_shape=jax.ShapeDtypeStruct((bsz, ne, cap, d), BF16),
        scratch_shapes=[pltpu.VMEM((cap, d), BF16), pltpu.VMEM((cap, 1), F32), pltpu.VMEM((cap, d), F32)],
        compiler_params=_cparams(("parallel", "arbitrary", "arbitrary")), name="moe_ffn")(
            pos4, gate4, hb3, w_gate, w_up, w_down)


def _combine_kernel(pos_ref, y_ref, h_ref, g_ref, b_ref, hn_ref, hb_ref, *, cap, group):
    tt = h_ref.shape[1]
    ne = y_ref.shape[1]
    d = y_ref.shape[-1]
    slot = lax.broadcasted_iota(I32, (group, cap, tt), 1)
    acc = jnp.zeros((tt, d), F32)
    for e0 in range(0, ne, group):
        onehot = (slot == pos_ref[0, e0:e0 + group]).astype(BF16).reshape(group * cap, tt)
        ys = y_ref[0, e0:e0 + group].reshape(group * cap, d)
        acc = acc + _dot_tn(onehot, ys)
    hn = _layer_norm(DN_ALPHA * h_ref[0] + acc, g_ref[...], b_ref[...])
    hn_ref[0] = hn
    hb_ref[0] = hn.astype(BF16)


def _combine(y, pos, h3, g, b):
    bsz, ne, cap, d = y.shape
    seq = h3.shape[1]
    tt = COMBINE_TOKENS
    pos4 = pos.reshape(bsz, ne, 1, seq)
    g2, b2 = g.reshape(1, -1).astype(F32), b.reshape(1, -1).astype(F32)
    hblk = pl.BlockSpec((1, tt, d), lambda bb, k: (bb, k, 0))
    vec = pl.BlockSpec((1, d), lambda bb, k: (0, 0))
    return pl.pallas_call(
        functools.partial(_combine_kernel, cap=cap, group=4), grid=(bsz, seq // tt),
        in_specs=[pl.BlockSpec((1, ne, 1, tt), lambda bb, k: (bb, 0, 0, k)),
                  pl.BlockSpec((1, ne, cap, d), lambda bb, k: (bb, 0, 0, 0), pipeline_mode=pl.Buffered(1)),
                  hblk, vec, vec],
        out_specs=[hblk, hblk],
        out_shape=[jax.ShapeDtypeStruct(h3.shape, F32), jax.ShapeDtypeStruct(h3.shape, BF16)],
        compiler_params=_cparams(("parallel", "arbitrary")), name="moe_combine_ln")(pos4, y, h3, g2, b2)


def _moe_block(h2, hb2, logits_t, w_gate, w_up, w_down, g, b, bsz, seq):
    cap = 2 * seq // N_EXPERTS
    pos, gate = _route(logits_t, cap)
    y = _ffn(hb2.reshape(bsz, seq, D_MODEL), pos, gate, w_gate, w_up, w_down, cap)
    hn, hnb = _combine(y, pos, h2.reshape(bsz, seq, D_MODEL), g, b)
    return hn.reshape(bsz * seq, D_MODEL), hnb.reshape(bsz * seq, D_MODEL)


def _qkv_kernel(x_ref, wqt_ref, wkv_ref, qt_ref, k_ref, v_ref):
    x = x_ref[...]
    w = C_HEADS * HEAD_DIM
    qt_ref[0] = (_dot_nt(wqt_ref[...], x) * (HEAD_DIM ** -0.5)).astype(BF16)
    k_ref[...] = _dot(x, wkv_ref[:, :w]).astype(BF16)
    v_ref[...] = _dot(x, wkv_ref[:, w:]).astype(BF16)


def _qkv(hb2, w_qkv, bsz, seq):
    m = hb2.shape[0]
    tm = ROW_TILE
    nseq = seq // tm
    w = C_HEADS * HEAD_DIM
    wqt = w_qkv[:, :w].T.astype(BF16)
    wkv = w_qkv[:, w:].astype(BF16)
    row = pl.BlockSpec((tm, w), lambda i: (i, 0))
    return pl.pallas_call(
        _qkv_kernel, grid=(m // tm,),
        in_specs=[pl.BlockSpec((tm, D_MODEL), lambda i: (i, 0)),
                  pl.BlockSpec((w, D_MODEL), lambda i: (0, 0)),
                  pl.BlockSpec((D_MODEL, 2 * w), lambda i: (0, 0))],
        out_specs=[pl.BlockSpec((1, w, tm), lambda i: (i // nseq, 0, i % nseq)), row, row],
        out_shape=[jax.ShapeDtypeStruct((bsz, w, seq), BF16),
                   jax.ShapeDtypeStruct((m, w), BF16), jax.ShapeDtypeStruct((m, w), BF16)],
        compiler_params=_cparams(("parallel",)), name="qkv_proj")(hb2, wqt, wkv)


def _na_kernel(qt_ref, k_ref, v_ref, t2_ref, o_ref, s_ref, *, rows):
    rb = pl.program_id(2)
    nq = C_QROWS * GRID_W
    nk = C_KROWS * GRID_W
    r0 = rb * C_QROWS
    kr0 = jnp.clip(r0 - C_WIN_H // 2, 0, rows - C_KROWS)
    kstart = pl.multiple_of(kr0 * GRID_W, GRID_W)
    ki = lax.broadcasted_iota(I32, (nk, nq), 0)
    qi = lax.broadcasted_iota(I32, (nk, nq), 1)
    qrow, qcol = r0 + qi // GRID_W, qi % GRID_W
    krow, kcol = kr0 + ki // GRID_W, ki % GRID_W
    rs = jnp.clip(qrow - C_WIN_H // 2, 0, rows - C_WIN_H)
    cs = jnp.clip(qcol - C_WIN_W // 2, 0, GRID_W - C_WIN_W)
    valid = (krow >= rs) & (krow < rs + C_WIN_H) & (kcol >= cs) & (kcol < cs + C_WIN_W)
    mask_add = jnp.where(valid, 0.0, NEG_INF).astype(F32)
    dbase = kr0 - r0 + C_WIN_H - 1
    feat = lax.broadcasted_iota(I32, (PAIR, nq), 0)
    for jp in range(C_PAIRS):
        psl = slice(jp * PAIR, (jp + 1) * PAIR)
        qt_pair = qt_ref[0, psl, :]
        k_pair = k_ref[0, pl.ds(kstart, nk), psl]
        for u in range(2):
            qm = jnp.where((feat >= HEAD_DIM) == (u == 1), qt_pair, jnp.zeros_like(qt_pair))
            s_ref[2 * jp + u] = _dot(k_pair, qm)
    for jp in range(C_PAIRS):
        psl = slice(jp * PAIR, (jp + 1) * PAIR)
        v_pair = v_ref[0, pl.ds(kstart, nk), psl]
        outs = []
        for u in range(2):
            h = 2 * jp + u
            bias_rows = []
            for i in range(C_KROWS):
                tiles = [t2_ref[h, jnp.clip(dbase + i - 2 * jq, 0, 2 * C_WIN_H - 1)] for jq in range(C_QROWS // 2)]
                bias_rows.append(jnp.concatenate(tiles, 1))
            s = s_ref[h] + jnp.concatenate(bias_rows, 0) + mask_add
            m = jnp.max(s, 0, keepdims=True)
            p = jnp.exp(s - m)
            l = jnp.sum(p, 0, keepdims=True)
            outs.append(_dot_tn(v_pair, p.astype(BF16)) / l)
        o_ref[0, psl, :] = jnp.where(feat < HEAD_DIM, outs[0], outs[1]).astype(BF16)


def _natten(qt3, k3, v3, rpb):
    bsz, seq, _ = k3.shape
    rows = seq // GRID_W
    hps = 2 * C_PAIRS
    cols = jnp.arange(GRID_W)
    dc = jnp.clip(cols[:, None] - cols[None, :] + C_WIN_W - 1, 0, 2 * C_WIN_W - 2)
    tt = rpb.astype(F32)[:, :, dc]
    zero = jnp.zeros_like(tt[:, :1])
    text = jnp.concatenate([zero, tt, zero], 1)
    t2 = jnp.concatenate([text[:, 1:], text[:, :-1]], -1)
    nd = t2.shape[1]
    nq = C_QROWS * GRID_W
    w = C_PAIRS * PAIR
    qblk = pl.BlockSpec((1, w, nq), lambda b, g, r: (b, g, r))
    kblk = pl.BlockSpec((1, seq, w), lambda b, g, r: (b, 0, g))
    return pl.pallas_call(
        functools.partial(_na_kernel, rows=rows),
        grid=(bsz, C_HEADS // hps, rows // C_QROWS),
        in_specs=[qblk, kblk, kblk,
                  pl.BlockSpec((hps, nd, GRID_W, 2 * GRID_W), lambda b, g, r: (g, 0, 0, 0))],
        out_specs=qblk, out_shape=jax.ShapeDtypeStruct((bsz, C_HEADS * HEAD_DIM, seq), BF16),
        scratch_shapes=[pltpu.VMEM((hps, C_KROWS * GRID_W, nq), F32)],
        compiler_params=_cparams(("parallel", "parallel", "arbitrary")), name="natten")(qt3, k3, v3, t2)


def kernel(x, w_in0, a_sink, mla_q_norm, w_q_up, mla_kv_norm, w_kv_up, w_out0, ln0a_g, ln0a_b,
           router0, w_gate0, w_up0, w_down0, ln0b_g, ln0b_b,
           w_qkv1, na_rpb, w_out1, ln1a_g, ln1a_b,
           router1, w_gate1, w_up1, w_down1, ln1b_g, ln1b_b):
    bsz, seq, d = x.shape
    m = bsz * seq
    x2 = x.reshape(m, d)
    qa, ka, va, qt, km, vt = _even_proj(x2, w_in0, mla_q_norm, w_q_up, mla_kv_norm, w_kv_up, bsz, seq)
    out_a = _swa(qa.reshape(bsz, seq, -1), ka.reshape(bsz, seq, -1), va.reshape(bsz, seq, -1), a_sink)
    out_bt = _mla(qt, km.reshape(bsz, seq, -1), vt)
    h, hb, lg = _outproj([out_a.reshape(m, A_Q)], [w_out0[:A_Q]], [out_bt], [w_out0[A_Q:]],
                         x2, ln0a_g, ln0a_b, router0, bsz, seq)
    h, hb = _moe_block(h, hb, lg, w_gate0, w_up0, w_down0, ln0b_g, ln0b_b, bsz, seq)
    qt1, k2, v2 = _qkv(hb, w_qkv1, bsz, seq)
    na_t = _natten(qt1, k2.reshape(bsz, seq, -1), v2.reshape(bsz, seq, -1), na_rpb)
    h, hb, lg = _outproj([], [], [na_t.reshape(bsz, C_HEADS, HEAD_DIM, seq)], [w_out1],
                         h, ln1a_g, ln1a_b, router1, bsz, seq)
    h, _ = _moe_block(h, hb, lg, w_gate1, w_up1, w_down1, ln1b_g, ln1b_b, bsz, seq)
    return h.reshape(bsz, seq, d)
```

```python
import functools
import math

import jax
import jax.numpy as jnp
from jax import lax
from jax.experimental import pallas as pl
from jax.experimental.pallas import tpu as pltpu

F32 = jnp.float32
BF16 = jnp.bfloat16
I32 = jnp.int32

D_MODEL = 1024
DEPTH = 2
HEAD_DIM = 64
NEG_INF = -1e30
A_HEADS = 8
A_KV_HEADS = 2
A_REP = A_HEADS // A_KV_HEADS
A_WINDOW = 128
A_Q = A_HEADS * HEAD_DIM
A_KV = A_KV_HEADS * HEAD_DIM
A_TQ = 256
B_HEADS = 8
B_Q_RANK = 384
B_KV_RANK = 256
B_NOPE = 64
B_ROPE = 32
B_V = 64
B_PAD = 128
B_TQ = 256
ROPE_BASE = 10000.0
C_HEADS = 16
GRID_W = 64
C_WIN_H = 8
C_WIN_W = 16
C_QROWS = 4
C_KROWS = 12
C_PAIRS = 4
N_EXPERTS = 16
D_EXPERT = 2 * D_MODEL
TOK_CHUNK = 256
GATHER_CHUNK = 1024
COMBINE_TOKENS = 1024
FFN_CHUNK = 1024
LN_EPS = 1e-5
RMS_EPS = 1e-6
DN_ALPHA = (2 * DEPTH) ** 0.25

LANES = 128
PAIR = 2 * HEAD_DIM
VMEM_LIMIT = 56 * 1024 * 1024
ROW_TILE = 512


def _cparams(sem):
    return pltpu.CompilerParams(dimension_semantics=sem, vmem_limit_bytes=VMEM_LIMIT)


def _layer_norm(z, g, b):
    mu = jnp.mean(z, -1, keepdims=True)
    zc = z - mu
    var = jnp.mean(zc * zc, -1, keepdims=True)
    return zc * lax.rsqrt(var + LN_EPS) * g + b


def _rms_norm(c, g):
    return c * lax.rsqrt(jnp.mean(c * c, -1, keepdims=True) + RMS_EPS) * g


def _bf16_part(x):
    return pltpu.bitcast(pltpu.bitcast(x, I32) & jnp.int32(-65536), F32)


def _dot(a, b):
    return jnp.dot(a, b, preferred_element_type=F32)


def _dot_nt(a, b):
    return lax.dot_general(a, b, (((1,), (1,)), ((), ())), preferred_element_type=F32)


def _dot_tn(a, b):
    return lax.dot_general(a, b, (((0,), (0,)), ((), ())), preferred_element_type=F32)


def _own_half(lo, hi):
    lane = lax.broadcasted_iota(I32, lo.shape, 1)
    return jnp.where(lane < HEAD_DIM, lo, hi)


def _even_proj_kernel(x_ref, wa_ref, wcq_ref, wckv_ref, wkp_ref, wkr_ref, gq_ref, gkv_ref,
                      wq1t_ref, wq2t_ref, wkn_ref, wvt_ref, cos_ref, sin_ref, cost_ref, sint_ref,
                      qa_ref, ka_ref, va_ref, qt_ref, k_ref, vt_ref, *, q_scale):
    xb = x_ref[...].astype(BF16)
    a = _dot(xb, wa_ref[...]).astype(BF16)
    qa_ref[...] = a[:, :A_Q]
    ka_ref[...] = a[:, A_Q:A_Q + 2 * A_KV]
    va_ref[...] = a[:, A_Q + 2 * A_KV:]

    nq = _rms_norm(_dot(xb, wcq_ref[...]), gq_ref[...]).astype(BF16)
    qpt = _dot_nt(wq1t_ref[...], nq)
    qrt = _dot_nt(wq2t_ref[...], nq)
    cost, sint = cost_ref[...], sint_ref[...]
    for h in range(B_HEADS):
        sl = slice(h * B_PAD, (h + 1) * B_PAD)
        qt_ref[0, h] = ((qpt[sl] * cost + qrt[sl] * sint) * q_scale).astype(BF16)

    nkv = _rms_norm(_dot(xb, wckv_ref[...]), gkv_ref[...]).astype(BF16)
    kn = _dot(nkv, wkn_ref[...])
    kpe = _dot(xb, wkp_ref[...]) * cos_ref[...] + _dot(xb, wkr_ref[...]) * sin_ref[...]
    for h in range(B_HEADS):
        sl = slice(h * B_PAD, (h + 1) * B_PAD)
        k_ref[:, sl] = (kn[:, sl] + kpe).astype(BF16)
    vt = _dot_nt(wvt_ref[...], nkv)
    vt_ref[0, :, 0] = vt.reshape(B_HEADS, B_V, vt.shape[-1]).astype(BF16)


def _even_proj(x2, w_in, gq, w_q_up, gkv, w_kv_up, bsz, seq):
    m = x2.shape[0]
    tm = ROW_TILE
    nseq = seq // tm
    o1, o2, o3 = A_Q, A_Q + A_KV, A_Q + 2 * A_KV
    o4, o5 = o3 + B_Q_RANK, o3 + B_Q_RANK + B_KV_RANK
    half = B_ROPE // 2
    wk = w_in[:, o1:o2].reshape(D_MODEL, A_KV_HEADS, 1, HEAD_DIM)
    wv = w_in[:, o2:o3].reshape(D_MODEL, A_KV_HEADS, 1, HEAD_DIM)
    dup = lambda w: jnp.broadcast_to(w, (D_MODEL, A_KV_HEADS, 2, HEAD_DIM)).reshape(D_MODEL, 2 * A_KV)
    wa = jnp.concatenate([w_in[:, :o1] * (HEAD_DIM ** -0.5), dup(wk), dup(wv)], 1).astype(BF16)
    wcq = w_in[:, o3:o4].astype(BF16)
    wckv = w_in[:, o4:o5].astype(BF16)
    wkr_raw = w_in[:, o5:]
    zpad = jnp.zeros((D_MODEL, B_PAD - B_NOPE - B_ROPE), F32)
    znope = jnp.zeros((D_MODEL, B_NOPE), F32)
    wkp = jnp.concatenate([znope, wkr_raw, zpad], 1).astype(BF16)
    wkr = jnp.concatenate([znope, -wkr_raw[:, half:], wkr_raw[:, :half], zpad], 1).astype(BF16)
    wq = w_q_up.reshape(B_Q_RANK, B_HEADS, B_NOPE + B_ROPE)
    wq_n, wq_r = wq[..., :B_NOPE], wq[..., B_NOPE:]
    zq = jnp.zeros((B_Q_RANK, B_HEADS, B_PAD - B_NOPE - B_ROPE), F32)
    wq1t = jnp.concatenate([wq_n, wq_r, zq], -1).reshape(B_Q_RANK, B_HEADS * B_PAD).T.astype(BF16)
    wq2t = jnp.concatenate([jnp.zeros_like(wq_n), -wq_r[..., half:], wq_r[..., :half], zq], -1)
    wq2t = wq2t.reshape(B_Q_RANK, B_HEADS * B_PAD).T.astype(BF16)
    wkv = w_kv_up.reshape(B_KV_RANK, B_HEADS, B_NOPE + B_V)
    zk = jnp.zeros((B_KV_RANK, B_HEADS, B_PAD - B_NOPE), F32)
    wkn = jnp.concatenate([wkv[..., :B_NOPE], zk], -1).reshape(B_KV_RANK, B_HEADS * B_PAD).astype(BF16)
    wvt = wkv[..., B_NOPE:].reshape(B_KV_RANK, B_HEADS * B_V).T.astype(BF16)
    inv = ROPE_BASE ** (-jnp.arange(half, dtype=F32) / half)
    inv_l = jnp.concatenate([jnp.zeros((B_NOPE,), F32), inv, inv, jnp.zeros((B_PAD - B_NOPE - B_ROPE,), F32)])
    ang = jnp.arange(seq, dtype=I32).astype(F32)[:, None] * inv_l[None, :]
    cos_t, sin_t = jnp.cos(ang), jnp.sin(ang)

    full = lambda a: pl.BlockSpec(a.shape, lambda i: (0,) * a.ndim)
    row = lambda w: pl.BlockSpec((tm, w), lambda i: (i, 0))
    tab = pl.BlockSpec((tm, B_PAD), lambda i: (i % nseq, 0))
    tab_t = pl.BlockSpec((B_PAD, tm), lambda i: (0, i % nseq))
    gq2, gkv2 = gq.reshape(1, -1).astype(F32), gkv.reshape(1, -1).astype(F32)
    ins = [x2, wa, wcq, wckv, wkp, wkr, gq2, gkv2, wq1t, wq2t, wkn, wvt, cos_t, sin_t, cos_t.T, sin_t.T]
    in_specs = [row(D_MODEL)] + [full(a) for a in ins[1:12]] + [tab, tab, tab_t, tab_t]
    out_shape = [jax.ShapeDtypeStruct((m, A_Q), BF16),
                 jax.ShapeDtypeStruct((m, 2 * A_KV), BF16), jax.ShapeDtypeStruct((m, 2 * A_KV), BF16),
                 jax.ShapeDtypeStruct((bsz, B_HEADS, B_PAD, seq), BF16),
                 jax.ShapeDtypeStruct((m, B_HEADS * B_PAD), BF16),
                 jax.ShapeDtypeStruct((bsz, B_HEADS, nseq, B_V, tm), BF16)]
    out_specs = [row(A_Q), row(2 * A_KV), row(2 * A_KV),
                 pl.BlockSpec((1, B_HEADS, B_PAD, tm), lambda i: (i // nseq, 0, 0, i % nseq)),
                 row(B_HEADS * B_PAD),
                 pl.BlockSpec((1, B_HEADS, 1, B_V, tm), lambda i: (i // nseq, 0, i % nseq, 0, 0))]
    q_scale = (B_NOPE + B_ROPE) ** -0.5 * math.log2(math.e)
    return pl.pallas_call(
        functools.partial(_even_proj_kernel, q_scale=q_scale),
        grid=(m // tm,), in_specs=in_specs, out_specs=out_specs, out_shape=out_shape,
        compiler_params=_cparams(("parallel",)), name="even_proj")(*ins)


def _swa_bias(start, seq):
    tq = A_TQ
    span = tq + 2 * A_WINDOW
    t = start + lax.broadcasted_iota(I32, (tq, span), 0)
    src = start - A_WINDOW + lax.broadcasted_iota(I32, (tq, span), 1)
    dist = jnp.abs(t - src)
    valid = (dist <= A_WINDOW) & (src >= 0) & (src < seq)
    slopes = 2.0 ** (-8.0 * (jnp.arange(A_HEADS, dtype=F32) + 1.0) / A_HEADS)
    return jnp.where(valid[None], -slopes[:, None, None] * dist.astype(F32)[None], NEG_INF)


def _swa_kernel(sink_ref, q_ref, kp_ref, kc_ref, kn_ref, vp_ref, vc_ref, vn_ref, bias_ref, o_ref):
    tq = A_TQ
    kb = jnp.concatenate([kp_ref[0], kc_ref[0], kn_ref[0]], 0)
    vb = jnp.concatenate([vp_ref[0], vc_ref[0], vn_ref[0]], 0)
    lane = lax.broadcasted_iota(I32, (tq, PAIR), 1)
    outs = []
    for h in range(A_HEADS):
        g = h // A_REP
        q_pair = q_ref[0, :, (h // 2) * PAIR:(h // 2 + 1) * PAIR]
        qm = jnp.where((lane >= HEAD_DIM) == (h % 2 == 1), q_pair, jnp.zeros_like(q_pair))
        s = _dot_nt(qm, kb[:, g * PAIR:(g + 1) * PAIR]) + bias_ref[0, h]
        sink = sink_ref[h]
        m = jnp.maximum(jnp.max(s, -1, keepdims=True), sink)
        p = jnp.exp(s - m)
        denom = jnp.sum(p, -1, keepdims=True) + jnp.exp(sink - m)
        outs.append(_dot(p.astype(BF16), vb[:, g * PAIR:(g + 1) * PAIR]) / denom)
    for jp in range(A_HEADS // 2):
        o_ref[0, :, jp * PAIR:(jp + 1) * PAIR] = _own_half(outs[2 * jp], outs[2 * jp + 1]).astype(BF16)


def _swa(qa3, ka3, va3, a_sink):
    bsz, seq, _ = qa3.shape
    tq, w = A_TQ, A_WINDOW
    nblk = seq // w
    r = tq // w
    kw = 2 * A_KV
    prev = pl.BlockSpec((1, w, kw), lambda b, i: (b, jnp.maximum(i * r - 1, 0), 0))
    cur = pl.BlockSpec((1, tq, kw), lambda b, i: (b, i, 0))
    nxt = pl.BlockSpec((1, w, kw), lambda b, i: (b, jnp.minimum(i * r + r, nblk - 1), 0))
    qblk = pl.BlockSpec((1, tq, A_Q), lambda b, i: (b, i, 0))
    nstep = seq // tq
    assert nstep >= 3
    bias = jnp.stack([_swa_bias(s0, seq) for s0 in (0, tq, seq - tq)])
    bblk = pl.BlockSpec((1, A_HEADS, tq, tq + 2 * w),
                        lambda b, i: ((i > 0).astype(I32) + (i == nstep - 1).astype(I32), 0, 0, 0))
    return pl.pallas_call(
        _swa_kernel, grid=(bsz, nstep),
        in_specs=[pl.BlockSpec(memory_space=pltpu.SMEM), qblk, prev, cur, nxt, prev, cur, nxt, bblk],
        out_specs=qblk, out_shape=jax.ShapeDtypeStruct((bsz, seq, A_Q), BF16),
        compiler_params=_cparams(("parallel", "arbitrary")), name="swa_attn")(
            a_sink.astype(F32), qa3, ka3, ka3, ka3, va3, va3, va3, bias)


def _mla_kernel(qt_ref, k_ref, vt_ref, o_ref, sa_ref, sb_ref, cma_ref, cmb_ref, m_ref, l_ref, acc_ref,
                *, tk, n_chunks):
    m_ref[...] = jnp.full_like(m_ref, NEG_INF)
    l_ref[...] = jnp.zeros_like(l_ref)
    acc_ref[...] = jnp.zeros_like(acc_ref)

    def scores(c, s_ref, cm_ref):
        rows = pl.ds(pl.multiple_of(c * tk, tk), tk)
        for h in range(B_HEADS):
            s = _dot(k_ref[0, rows, h * B_PAD:(h + 1) * B_PAD], qt_ref[0, h])
            s_ref[h] = s
            cm_ref[h] = jnp.max(s, 0, keepdims=True)

    def update(c, s_ref, cm_ref):
        for h in range(B_HEADS):
            m_old = m_ref[h]
            m_new = jnp.maximum(m_old, cm_ref[h])
            alpha = jnp.exp2(m_old - m_new)
            p = jnp.exp2(s_ref[h] - m_new)
            l_ref[h] = l_ref[h] * alpha + jnp.sum(p, 0, keepdims=True)
            acc_ref[h] = acc_ref[h] * alpha + _dot(vt_ref[0, h, c], p.astype(BF16))
            m_ref[h] = m_new

    a, b = (sa_ref, cma_ref), (sb_ref, cmb_ref)
    scores(0, *a)

    def body(i, _):
        scores(2 * i + 1, *b)
        update(2 * i, *a)
        scores(2 * i + 2, *a)
        update(2 * i + 1, *b)
        return 0

    lax.fori_loop(0, n_chunks // 2 - 1, body, 0)
    scores(n_chunks - 1, *b)
    update(n_chunks - 2, *a)
    update(n_chunks - 1, *b)
    for h in range(B_HEADS):
        o_ref[0, h] = (acc_ref[h] / l_ref[h]).astype(BF16)


def _mla(qt, k3, vt):
    bsz, _, _, seq = qt.shape
    tq = B_TQ
    nck, tk = vt.shape[2], vt.shape[4]
    return pl.pallas_call(
        functools.partial(_mla_kernel, tk=tk, n_chunks=nck),
        grid=(bsz, seq // tq),
        in_specs=[pl.BlockSpec((1, B_HEADS, B_PAD, tq), lambda b, i: (b, 0, 0, i)),
                  pl.BlockSpec((1, seq, B_HEADS * B_PAD), lambda b, i: (b, 0, 0)),
                  pl.BlockSpec((1, B_HEADS, nck, B_V, tk), lambda b, i: (b, 0, 0, 0, 0))],
        out_specs=pl.BlockSpec((1, B_HEADS, B_V, tq), lambda b, i: (b, 0, 0, i)),
        out_shape=jax.ShapeDtypeStruct((bsz, B_HEADS, B_V, seq), BF16),
        scratch_shapes=[pltpu.VMEM((B_HEADS, tk, tq), F32), pltpu.VMEM((B_HEADS, tk, tq), F32),
                        pltpu.VMEM((B_HEADS, 1, tq), F32), pltpu.VMEM((B_HEADS, 1, tq), F32),
                        pltpu.VMEM((B_HEADS, 1, tq), F32), pltpu.VMEM((B_HEADS, 1, tq), F32),
                        pltpu.VMEM((B_HEADS, B_V, tq), F32)],
        compiler_params=_cparams(("parallel", "arbitrary")), name="mla_attn")(qt, k3, vt)


def _outproj_kernel(*refs, n_row, n_t):
    n_in = n_row + n_t
    a_refs = refs[:n_in]
    w_refs = refs[n_in:2 * n_in]
    h_ref, g_ref, b_ref, wr_ref = refs[2 * n_in:2 * n_in + 4]
    hn_ref, hb_ref, lg_ref = refs[2 * n_in + 4:]
    mix = None
    for a_ref, w_ref in zip(a_refs[:n_row], w_refs[:n_row]):
        part = _dot(a_ref[...], w_ref[...])
        mix = part if mix is None else mix + part
    for a_ref, w_ref in zip(a_refs[n_row:], w_refs[n_row:]):
        at = a_ref[0]
        part = _dot_tn(at.reshape(at.shape[0] * at.shape[1], at.shape[2]), w_ref[...])
        mix = part if mix is None else mix + part
    hn = _layer_norm(DN_ALPHA * h_ref[...] + mix, g_ref[...], b_ref[...])
    hn_ref[...] = hn
    hb_ref[...] = hn.astype(BF16)
    hi_f = _bf16_part(hn)
    whl = wr_ref[...]
    parts = _dot_nt(whl, hi_f.astype(BF16)) + _dot_nt(whl, (hn - hi_f).astype(BF16))
    lg_ref[0] = parts[:N_EXPERTS] + parts[N_EXPERTS:]


def _outproj(rows, rows_w, trans, trans_w, h2, g, b, w_router, bsz, seq):
    m = h2.shape[0]
    tm = ROW_TILE
    nseq = seq // tm
    w_list = [w.astype(BF16) for w in list(rows_w) + list(trans_w)]
    wr_f = w_router.T.astype(F32)
    wr_hi = lax.bitcast_convert_type(lax.bitcast_convert_type(wr_f, jnp.uint32) & jnp.uint32(0xFFFF0000), F32)
    wr_t = jnp.concatenate([wr_hi, wr_f - wr_hi], 0).astype(BF16)
    row = lambda w: pl.BlockSpec((tm, w), lambda i: (i, 0))
    full = lambda a: pl.BlockSpec(a.shape, lambda i: (0,) * a.ndim)
    tblk = lambda a: pl.BlockSpec((1, a.shape[1], a.shape[2], tm), lambda i: (i // nseq, 0, 0, i % nseq))
    g2, b2 = g.reshape(1, -1).astype(F32), b.reshape(1, -1).astype(F32)
    ins = list(rows) + list(trans) + w_list + [h2, g2, b2, wr_t]
    in_specs = ([row(a.shape[1]) for a in rows] + [tblk(a) for a in trans] + [full(w) for w in w_list]
                + [row(D_MODEL), full(g2), full(b2), full(wr_t)])
    out_shape = [jax.ShapeDtypeStruct((m, D_MODEL), F32), jax.ShapeDtypeStruct((m, D_MODEL), BF16),
                 jax.ShapeDtypeStruct((bsz, N_EXPERTS, seq), F32)]
    out_specs = [row(D_MODEL), row(D_MODEL),
                 pl.BlockSpec((1, N_EXPERTS, tm), lambda i: (i // nseq, 0, i % nseq))]
    return pl.pallas_call(
        functools.partial(_outproj_kernel, n_row=len(rows), n_t=len(trans)), grid=(m // tm,),
        in_specs=in_specs, out_specs=out_specs, out_shape=out_shape,
        compiler_params=_cparams(("parallel",)), name="outproj_ln")(*ins)


def _route_kernel(lg_ref, pos_ref, gate_ref, *, cap, seq):
    lg = lg_ref[0]
    mx = jnp.max(lg, 0, keepdims=True)
    ex = jnp.exp(lg - mx)
    aff = ex / jnp.sum(ex, 0, keepdims=True)
    gate_ref[0] = aff

    def bis(_, carry):
        lo, hi = carry
        mid = lo + ((hi - lo + 1) >> 1)
        cnt = jnp.sum((aff >= pltpu.bitcast(mid, F32)).astype(I32), 1, keepdims=True)
        ok = cnt >= cap
        return jnp.where(ok, mid, lo), jnp.where(ok, hi, mid - 1)

    lo0 = jnp.zeros((N_EXPERTS, 1), I32)
    hi0 = jnp.full((N_EXPERTS, 1), 0x7F800000, I32)
    thr, _ = lax.fori_loop(0, 32, bis, (lo0, hi0))
    gt = aff >= pltpu.bitcast(thr + 1, F32)
    eq = (aff >= pltpu.bitcast(thr, F32)) & jnp.logical_not(gt)
    need = cap - jnp.sum(gt.astype(I32), 1, keepdims=True)

    ck = TOK_CHUNK
    nck = seq // ck
    tri = (lax.broadcasted_iota(I32, (ck, ck), 0) <= lax.broadcasted_iota(I32, (ck, ck), 1)).astype(BF16)

    def chunk_cumsum(mask, k, run):
        mk = mask[:, k * ck:(k + 1) * ck]
        inc = _dot(mk.astype(BF16), tri).astype(I32)
        return mk, inc + run, run + inc[:, ck - 1:ck]

    run = jnp.zeros((N_EXPERTS, 1), I32)
    sel_chunks = []
    for k in range(nck):
        mk, inc, run = chunk_cumsum(eq, k, run)
        sel_chunks.append(gt[:, k * ck:(k + 1) * ck] | (mk & (inc <= need)))

    run = jnp.zeros((N_EXPERTS, 1), I32)
    for k in range(nck):
        mk = sel_chunks[k]
        inc = _dot(mk.astype(BF16), tri).astype(I32) + run
        pos_ref[0, :, k * ck:(k + 1) * ck] = jnp.where(mk, inc - 1, -1)
        run = inc[:, ck - 1:ck]


def _route(logits_t, cap):
    bsz, _, seq = logits_t.shape
    blk = pl.BlockSpec((1, N_EXPERTS, seq), lambda b: (b, 0, 0))
    return pl.pallas_call(
        functools.partial(_route_kernel, cap=cap, seq=seq), grid=(bsz,),
        in_specs=[blk], out_specs=[blk, blk],
        out_shape=[jax.ShapeDtypeStruct((bsz, N_EXPERTS, seq), I32),
                   jax.ShapeDtypeStruct((bsz, N_EXPERTS, seq), F32)],
        compiler_params=_cparams(("parallel",)), name="route_topk")(logits_t)


def _ffn_kernel(pos_ref, gate_ref, h_ref, wg_ref, wu_ref, wd_ref, y_ref, x_ref, gs_ref, acc_ref, *, cap, nck):
    fc = pl.program_id(2)

    @pl.when(fc == 0)
    def _():
        acc_ref[...] = jnp.zeros_like(acc_ref)
        gs_ref[...] = jnp.zeros_like(gs_ref)
        slot = lax.broadcasted_iota(I32, (cap, GATHER_CHUNK), 0)

        def gather(k, _):
            match = slot == pos_ref[0, 0, pl.ds(k, 1), :]
            rows = pl.ds(pl.multiple_of(k * GATHER_CHUNK, GATHER_CHUNK), GATHER_CHUNK)
            acc_ref[...] += _dot(match.astype(BF16), h_ref[0, rows, :])
            gs_ref[...] += jnp.sum(jnp.where(match, gate_ref[0, 0, pl.ds(k, 1), :], 0.0), 1, keepdims=True)
            return 0

        lax.fori_loop(0, nck, gather, 0)
        x_ref[...] = acc_ref[...].astype(BF16)
        acc_ref[...] = jnp.zeros_like(acc_ref)

    x = x_ref[...]
    g = _dot(x, wg_ref[0].astype(BF16))
    u = _dot(x, wu_ref[0].astype(BF16))
    hid = (g * jax.nn.sigmoid(g) * u).astype(BF16)
    acc_ref[...] += _dot(hid, wd_ref[0].astype(BF16))

    @pl.when(fc == pl.num_programs(2) - 1)
    def _():
        y_ref[0, 0] = (acc_ref[...] * gs_ref[...]).astype(BF16)


def _ffn(hb3, pos, gate, w_gate, w_up, w_down, cap):
    bsz, seq, d = hb3.shape
    ne, f = w_gate.shape[0], w_gate.shape[-1]
    nck = seq // GATHER_CHUNK
    fck = FFN_CHUNK
    pos4 = pos.reshape(bsz, ne, nck, GATHER_CHUNK)
    gate4 = gate.reshape(bsz, ne, nck, GATHER_CHUNK)
    tok = pl.BlockSpec((1, 1, nck, GATHER_CHUNK), lambda b, e, c: (b, e, 0, 0))
    return pl.pallas_call(
        functools.partial(_ffn_kernel, cap=cap, nck=nck), grid=(bsz, ne, f // fck),
        in_specs=[tok, tok,
                  pl.BlockSpec((1, seq, d), lambda b, e, c: (b, 0, 0)),
                  pl.BlockSpec((1, d, fck), lambda b, e, c: (e, 0, c)),
                  pl.BlockSpec((1, d, fck), lambda b, e, c: (e, 0, c)),
                  pl.BlockSpec((1, fck, d), lambda b, e, c: (e, c, 0))],
        out_specs=pl.BlockSpec((1, 1, cap, d), lambda b, e, c: (b, e, 0, 0)),
        out_shape=jax.ShapeDtypeStruct((bsz, ne, cap, d), BF16),
        scratch_shapes=[pltpu.VMEM((cap, d), BF16), pltpu.VMEM((cap, 1), F32), pltpu.VMEM((cap, d), F32)],
        compiler_params=_cparams(("parallel", "arbitrary", "arbitrary")), name="moe_ffn")(
            pos4, gate4, hb3, w_gate, w_up, w_down)


def _combine_kernel(pos_ref, y_ref, h_ref, g_ref, b_ref, hn_ref, hb_ref, *, cap, group):
    tt = h_ref.shape[1]
    ne = y_ref.shape[1]
    d = y_ref.shape[-1]
    slot = lax.broadcasted_iota(I32, (group, cap, tt), 1)
    acc = jnp.zeros((tt, d), F32)
    for e0 in range(0, ne, group):
        onehot = (slot == pos_ref[0, e0:e0 + group]).astype(BF16).reshape(group * cap, tt)
        ys = y_ref[0, e0:e0 + group].reshape(group * cap, d)
        acc = acc + _dot_tn(onehot, ys)
    hn = _layer_norm(DN_ALPHA * h_ref[0] + acc, g_ref[...], b_ref[...])
    hn_ref[0] = hn
    hb_ref[0] = hn.astype(BF16)


def _combine(y, pos, h3, g, b):
    bsz, ne, cap, d = y.shape
    seq = h3.shape[1]
    tt = COMBINE_TOKENS
    pos4 = pos.reshape(bsz, ne, 1, seq)
    g2, b2 = g.reshape(1, -1).astype(F32), b.reshape(1, -1).astype(F32)
    hblk = pl.BlockSpec((1, tt, d), lambda bb, k: (bb, k, 0))
    vec = pl.BlockSpec((1, d), lambda bb, k: (0, 0))
    return pl.pallas_call(
        functools.partial(_combine_kernel, cap=cap, group=4), grid=(bsz, seq // tt),
        in_specs=[pl.BlockSpec((1, ne, 1, tt), lambda bb, k: (bb, 0, 0, k)),
                  pl.BlockSpec((1, ne, cap, d), lambda bb, k: (bb, 0, 0, 0), pipeline_mode=pl.Buffered(1)),
                  hblk, vec, vec],
        out_specs=[hblk, hblk],
        out_shape=[jax.ShapeDtypeStruct(h3.shape, F32), jax.ShapeDtypeStruct(h3.shape, BF16)],
        compiler_params=_cparams(("parallel", "arbitrary")), name="moe_combine_ln")(pos4, y, h3, g2, b2)


def _moe_block(h2, hb2, logits_t, w_gate, w_up, w_down, g, b, bsz, seq):
    cap = 2 * seq // N_EXPERTS
    pos, gate = _route(logits_t, cap)
    y = _ffn(hb2.reshape(bsz, seq, D_MODEL), pos, gate, w_gate, w_up, w_down, cap)
    hn, hnb = _combine(y, pos, h2.reshape(bsz, seq, D_MODEL), g, b)
    return hn.reshape(bsz * seq, D_MODEL), hnb.reshape(bsz * seq, D_MODEL)


def _qkv_kernel(x_ref, wqt_ref, wkv_ref, qt_ref, k_ref, v_ref):
    x = x_ref[...]
    w = C_HEADS * HEAD_DIM
    qt_ref[0] = (_dot_nt(wqt_ref[...], x) * (HEAD_DIM ** -0.5)).astype(BF16)
    k_ref[...] = _dot(x, wkv_ref[:, :w]).astype(BF16)
    v_ref[...] = _dot(x, wkv_ref[:, w:]).astype(BF16)


def _qkv(hb2, w_qkv, bsz, seq):
    m = hb2.shape[0]
    tm = ROW_TILE
    nseq = seq // tm
    w = C_HEADS * HEAD_DIM
    wqt = w_qkv[:, :w].T.astype(BF16)
    wkv = w_qkv[:, w:].astype(BF16)
    row = pl.BlockSpec((tm, w), lambda i: (i, 0))
    return pl.pallas_call(
        _qkv_kernel, grid=(m // tm,),
        in_specs=[pl.BlockSpec((tm, D_MODEL), lambda i: (i, 0)),
                  pl.BlockSpec((w, D_MODEL), lambda i: (0, 0)),
                  pl.BlockSpec((D_MODEL, 2 * w), lambda i: (0, 0))],
        out_specs=[pl.BlockSpec((1, w, tm), lambda i: (i // nseq, 0, i % nseq)), row, row],
        out_shape=[jax.ShapeDtypeStruct((bsz, w, seq), BF16),
                   jax.ShapeDtypeStruct((m, w), BF16), jax.ShapeDtypeStruct((m, w), BF16)],
        compiler_params=_cparams(("parallel",)), name="qkv_proj")(hb2, wqt, wkv)


def _na_window_mask(r0, rows):
    nq, nk = C_QROWS * GRID_W, C_KROWS * GRID_W
    kr0 = jnp.clip(r0 - C_WIN_H // 2, 0, rows - C_KROWS)
    ki = lax.broadcasted_iota(I32, (nk, nq), 0)
    qi = lax.broadcasted_iota(I32, (nk, nq), 1)
    qrow, qcol = r0 + qi // GRID_W, qi % GRID_W
    krow, kcol = kr0 + ki // GRID_W, ki % GRID_W
    rs = jnp.clip(qrow - C_WIN_H // 2, 0, rows - C_WIN_H)
    cs = jnp.clip(qcol - C_WIN_W // 2, 0, GRID_W - C_WIN_W)
    valid = (krow >= rs) & (krow < rs + C_WIN_H) & (kcol >= cs) & (kcol < cs + C_WIN_W)
    return jnp.where(valid, 0.0, NEG_INF).astype(F32)


def _na_kernel(qt_ref, k_ref, v_ref, t2_ref, mask_ref, o_ref, s_ref, *, rows):
    rb = pl.program_id(2)
    nq = C_QROWS * GRID_W
    nk = C_KROWS * GRID_W
    r0 = rb * C_QROWS
    kr0 = jnp.clip(r0 - C_WIN_H // 2, 0, rows - C_KROWS)
    kstart = pl.multiple_of(kr0 * GRID_W, GRID_W)
    mask_add = mask_ref[0]
    dbase = kr0 - r0 + C_WIN_H - 1
    feat = lax.broadcasted_iota(I32, (PAIR, nq), 0)
    for jp in range(C_PAIRS):
        psl = slice(jp * PAIR, (jp + 1) * PAIR)
        qt_pair = qt_ref[0, psl, :]
        k_pair = k_ref[0, pl.ds(kstart, nk), psl]
        for u in range(2):
            qm = jnp.where((feat >= HEAD_DIM) == (u == 1), qt_pair, jnp.zeros_like(qt_pair))
            s_ref[2 * jp + u] = _dot(k_pair, qm)
    for jp in range(C_PAIRS):
        psl = slice(jp * PAIR, (jp + 1) * PAIR)
        v_pair = v_ref[0, pl.ds(kstart, nk), psl]
        outs = []
        for u in range(2):
            h = 2 * jp + u
            bias_rows = []
            for i in range(C_KROWS):
                tiles = [t2_ref[h, jnp.clip(dbase + i - 2 * jq, 0, 2 * C_WIN_H - 1)] for jq in range(C_QROWS // 2)]
                bias_rows.append(jnp.concatenate(tiles, 1))
            s = s_ref[h] + jnp.concatenate(bias_rows, 0) + mask_add
            m = jnp.max(s, 0, keepdims=True)
            p = jnp.exp(s - m)
            l = jnp.sum(p, 0, keepdims=True)
            outs.append(_dot_tn(v_pair, p.astype(BF16)) / l)
        o_ref[0, psl, :] = jnp.where(feat < HEAD_DIM, outs[0], outs[1]).astype(BF16)


def _natten(qt3, k3, v3, rpb):
    bsz, seq, _ = k3.shape
    rows = seq // GRID_W
    hps = 2 * C_PAIRS
    cols = jnp.arange(GRID_W)
    dc = jnp.clip(cols[:, None] - cols[None, :] + C_WIN_W - 1, 0, 2 * C_WIN_W - 2)
    tt = rpb.astype(F32)[:, :, dc]
    zero = jnp.zeros_like(tt[:, :1])
    text = jnp.concatenate([zero, tt, zero], 1)
    t2 = jnp.concatenate([text[:, 1:], text[:, :-1]], -1)
    nd = t2.shape[1]
    nq = C_QROWS * GRID_W
    nk = C_KROWS * GRID_W
    nrb = rows // C_QROWS
    assert nrb >= 3
    masks = jnp.stack([_na_window_mask(r0, rows) for r0 in (0, C_QROWS, rows - C_QROWS)])
    w = C_PAIRS * PAIR
    qblk = pl.BlockSpec((1, w, nq), lambda b, g, r: (b, g, r))
    kblk = pl.BlockSpec((1, seq, w), lambda b, g, r: (b, 0, g))
    return pl.pallas_call(
        functools.partial(_na_kernel, rows=rows),
        grid=(bsz, C_HEADS // hps, nrb),
        in_specs=[qblk, kblk, kblk,
                  pl.BlockSpec((hps, nd, GRID_W, 2 * GRID_W), lambda b, g, r: (g, 0, 0, 0)),
                  pl.BlockSpec((1, nk, nq), lambda b, g, r: ((r > 0).astype(I32) + (r == nrb - 1).astype(I32), 0, 0))],
        out_specs=qblk, out_shape=jax.ShapeDtypeStruct((bsz, C_HEADS * HEAD_DIM, seq), BF16),
        scratch_shapes=[pltpu.VMEM((hps, C_KROWS * GRID_W, nq), F32)],
        compiler_params=_cparams(("parallel", "parallel", "arbitrary")), name="natten")(qt3, k3, v3, t2, masks)


def kernel(x, w_in0, a_sink, mla_q_norm, w_q_up, mla_kv_norm, w_kv_up, w_out0, ln0a_g, ln0a_b,
           router0, w_gate0, w_up0, w_down0, ln0b_g, ln0b_b,
           w_qkv1, na_rpb, w_out1, ln1a_g, ln1a_b,
           router1, w_gate1, w_up1, w_down1, ln1b_g, ln1b_b):
    bsz, seq, d = x.shape
    m = bsz * seq
    x2 = x.reshape(m, d)
    qa, ka, va, qt, km, vt = _even_proj(x2, w_in0, mla_q_norm, w_q_up, mla_kv_norm, w_kv_up, bsz, seq)
    out_a = _swa(qa.reshape(bsz, seq, -1), ka.reshape(bsz, seq, -1), va.reshape(bsz, seq, -1), a_sink)
    out_bt = _mla(qt, km.reshape(bsz, seq, -1), vt)
    h, hb, lg = _outproj([out_a.reshape(m, A_Q)], [w_out0[:A_Q]], [out_bt], [w_out0[A_Q:]],
                         x2, ln0a_g, ln0a_b, router0, bsz, seq)
    h, hb = _moe_block(h, hb, lg, w_gate0, w_up0, w_down0, ln0b_g, ln0b_b, bsz, seq)
    qt1, k2, v2 = _qkv(hb, w_qkv1, bsz, seq)
    na_t = _natten(qt1, k2.reshape(bsz, seq, -1), v2.reshape(bsz, seq, -1), na_rpb)
    h, hb, lg = _outproj([], [], [na_t.reshape(bsz, C_HEADS, HEAD_DIM, seq)], [w_out1],
                         h, ln1a_g, ln1a_b, router1, bsz, seq)
    h, _ = _moe_block(h, hb, lg, w_gate1, w_up1, w_down1, ln1b_g, ln1b_b, bsz, seq)
    return h.reshape(bsz, seq, d)
```

```python
import functools
import math

import jax
import jax.numpy as jnp
from jax import lax
from jax.experimental import pallas as pl
from jax.experimental.pallas import tpu as pltpu

F32 = jnp.float32
BF16 = jnp.bfloat16
I32 = jnp.int32

D_MODEL = 1024
DEPTH = 2
HEAD_DIM = 64
NEG_INF = -1e30
A_HEADS = 8
A_KV_HEADS = 2
A_REP = A_HEADS // A_KV_HEADS
A_WINDOW = 128
A_Q = A_HEADS * HEAD_DIM
A_KV = A_KV_HEADS * HEAD_DIM
A_TQ = 256
B_HEADS = 8
B_Q_RANK = 384
B_KV_RANK = 256
B_NOPE = 64
B_ROPE = 32
B_V = 64
B_PAD = 128
B_TQ = 512
ROPE_BASE = 10000.0
C_HEADS = 16
GRID_W = 64
C_WIN_H = 8
C_WIN_W = 16
C_QROWS = 4
C_KROWS = 12
C_PAIRS = 4
N_EXPERTS = 16
D_EXPERT = 2 * D_MODEL
TOK_CHUNK = 256
GATHER_CHUNK = 1024
COMBINE_TOKENS = 1024
FFN_CHUNK = 1024
LN_EPS = 1e-5
RMS_EPS = 1e-6
DN_ALPHA = (2 * DEPTH) ** 0.25

LOG2E = math.log2(math.e)
LANES = 128
PAIR = 2 * HEAD_DIM
VMEM_LIMIT = 56 * 1024 * 1024
ROW_TILE = 512


def _cparams(sem):
    return pltpu.CompilerParams(dimension_semantics=sem, vmem_limit_bytes=VMEM_LIMIT)


def _layer_norm(z, g, b):
    mu = jnp.mean(z, -1, keepdims=True)
    zc = z - mu
    var = jnp.mean(zc * zc, -1, keepdims=True)
    return zc * lax.rsqrt(var + LN_EPS) * g + b


def _rms_norm(c, g):
    return c * lax.rsqrt(jnp.mean(c * c, -1, keepdims=True) + RMS_EPS) * g


def _bf16_part(x):
    return pltpu.bitcast(pltpu.bitcast(x, I32) & jnp.int32(-65536), F32)


def _dot(a, b):
    return jnp.dot(a, b, preferred_element_type=F32)


def _dot_nt(a, b):
    return lax.dot_general(a, b, (((1,), (1,)), ((), ())), preferred_element_type=F32)


def _dot_tn(a, b):
    return lax.dot_general(a, b, (((0,), (0,)), ((), ())), preferred_element_type=F32)


def _own_half(lo, hi):
    lane = lax.broadcasted_iota(I32, lo.shape, 1)
    return jnp.where(lane < HEAD_DIM, lo, hi)


def _even_proj_kernel(x_ref, wa_ref, wcq_ref, wckv_ref, wkp_ref, wkr_ref, gq_ref, gkv_ref,
                      wq1t_ref, wq2t_ref, wkn_ref, wvt_ref, cos_ref, sin_ref, cost_ref, sint_ref,
                      qa_ref, ka_ref, va_ref, qt_ref, k_ref, vt_ref, *, q_scale):
    xb = x_ref[...].astype(BF16)
    a = _dot(xb, wa_ref[...])
    qa_ref[...] = (a[:, :A_Q] * (HEAD_DIM ** -0.5 * LOG2E)).astype(BF16)
    ka_ref[...] = a[:, A_Q:A_Q + 2 * A_KV].astype(BF16)
    va_ref[...] = a[:, A_Q + 2 * A_KV:].astype(BF16)

    nq = _rms_norm(_dot(xb, wcq_ref[...]), gq_ref[...]).astype(BF16)
    qpt = _dot_nt(wq1t_ref[...], nq)
    qrt = _dot_nt(wq2t_ref[...], nq)
    cost, sint = cost_ref[...], sint_ref[...]
    for h in range(B_HEADS):
        sl = slice(h * B_PAD, (h + 1) * B_PAD)
        qt_ref[0, h] = ((qpt[sl] * cost + qrt[sl] * sint) * q_scale).astype(BF16)

    nkv = _rms_norm(_dot(xb, wckv_ref[...]), gkv_ref[...]).astype(BF16)
    kn = _dot(nkv, wkn_ref[...])
    kpe = _dot(xb, wkp_ref[...]) * cos_ref[...] + _dot(xb, wkr_ref[...]) * sin_ref[...]
    for h in range(B_HEADS):
        sl = slice(h * B_PAD, (h + 1) * B_PAD)
        k_ref[:, sl] = (kn[:, sl] + kpe).astype(BF16)
    vt = _dot_nt(wvt_ref[...], nkv)
    vt_ref[0, :, 0] = vt.reshape(B_HEADS, B_V, vt.shape[-1]).astype(BF16)


def _even_proj(x2, w_in, gq, w_q_up, gkv, w_kv_up, bsz, seq):
    m = x2.shape[0]
    tm = ROW_TILE
    nseq = seq // tm
    o1, o2, o3 = A_Q, A_Q + A_KV, A_Q + 2 * A_KV
    o4, o5 = o3 + B_Q_RANK, o3 + B_Q_RANK + B_KV_RANK
    half = B_ROPE // 2
    wk = w_in[:, o1:o2].reshape(D_MODEL, A_KV_HEADS, 1, HEAD_DIM)
    wv = w_in[:, o2:o3].reshape(D_MODEL, A_KV_HEADS, 1, HEAD_DIM)
    dup = lambda w: jnp.broadcast_to(w, (D_MODEL, A_KV_HEADS, 2, HEAD_DIM)).reshape(D_MODEL, 2 * A_KV)
    wa = jnp.concatenate([w_in[:, :o1], dup(wk), dup(wv)], 1).astype(BF16)
    wcq = w_in[:, o3:o4].astype(BF16)
    wckv = w_in[:, o4:o5].astype(BF16)
    wkr_raw = w_in[:, o5:]
    zpad = jnp.zeros((D_MODEL, B_PAD - B_NOPE - B_ROPE), F32)
    znope = jnp.zeros((D_MODEL, B_NOPE), F32)
    wkp = jnp.concatenate([znope, wkr_raw, zpad], 1).astype(BF16)
    wkr = jnp.concatenate([znope, -wkr_raw[:, half:], wkr_raw[:, :half], zpad], 1).astype(BF16)
    wq = w_q_up.reshape(B_Q_RANK, B_HEADS, B_NOPE + B_ROPE)
    wq_n, wq_r = wq[..., :B_NOPE], wq[..., B_NOPE:]
    zq = jnp.zeros((B_Q_RANK, B_HEADS, B_PAD - B_NOPE - B_ROPE), F32)
    wq1t = jnp.concatenate([wq_n, wq_r, zq], -1).reshape(B_Q_RANK, B_HEADS * B_PAD).T.astype(BF16)
    wq2t = jnp.concatenate([jnp.zeros_like(wq_n), -wq_r[..., half:], wq_r[..., :half], zq], -1)
    wq2t = wq2t.reshape(B_Q_RANK, B_HEADS * B_PAD).T.astype(BF16)
    wkv = w_kv_up.reshape(B_KV_RANK, B_HEADS, B_NOPE + B_V)
    zk = jnp.zeros((B_KV_RANK, B_HEADS, B_PAD - B_NOPE), F32)
    wkn = jnp.concatenate([wkv[..., :B_NOPE], zk], -1).reshape(B_KV_RANK, B_HEADS * B_PAD).astype(BF16)
    wvt = wkv[..., B_NOPE:].reshape(B_KV_RANK, B_HEADS * B_V).T.astype(BF16)
    inv = ROPE_BASE ** (-jnp.arange(half, dtype=F32) / half)
    inv_l = jnp.concatenate([jnp.zeros((B_NOPE,), F32), inv, inv, jnp.zeros((B_PAD - B_NOPE - B_ROPE,), F32)])
    ang = jnp.arange(seq, dtype=I32).astype(F32)[:, None] * inv_l[None, :]
    cos_t, sin_t = jnp.cos(ang), jnp.sin(ang)

    full = lambda a: pl.BlockSpec(a.shape, lambda i: (0,) * a.ndim)
    row = lambda w: pl.BlockSpec((tm, w), lambda i: (i, 0))
    tab = pl.BlockSpec((tm, B_PAD), lambda i: (i % nseq, 0))
    tab_t = pl.BlockSpec((B_PAD, tm), lambda i: (0, i % nseq))
    gq2, gkv2 = gq.reshape(1, -1).astype(F32), gkv.reshape(1, -1).astype(F32)
    ins = [x2, wa, wcq, wckv, wkp, wkr, gq2, gkv2, wq1t, wq2t, wkn, wvt, cos_t, sin_t, cos_t.T, sin_t.T]
    in_specs = [row(D_MODEL)] + [full(a) for a in ins[1:12]] + [tab, tab, tab_t, tab_t]
    out_shape = [jax.ShapeDtypeStruct((m, A_Q), BF16),
                 jax.ShapeDtypeStruct((m, 2 * A_KV), BF16), jax.ShapeDtypeStruct((m, 2 * A_KV), BF16),
                 jax.ShapeDtypeStruct((bsz, B_HEADS, B_PAD, seq), BF16),
                 jax.ShapeDtypeStruct((m, B_HEADS * B_PAD), BF16),
                 jax.ShapeDtypeStruct((bsz, B_HEADS, nseq, B_V, tm), BF16)]
    out_specs = [row(A_Q), row(2 * A_KV), row(2 * A_KV),
                 pl.BlockSpec((1, B_HEADS, B_PAD, tm), lambda i: (i // nseq, 0, 0, i % nseq)),
                 row(B_HEADS * B_PAD),
                 pl.BlockSpec((1, B_HEADS, 1, B_V, tm), lambda i: (i // nseq, 0, i % nseq, 0, 0))]
    q_scale = (B_NOPE + B_ROPE) ** -0.5 * LOG2E
    return pl.pallas_call(
        functools.partial(_even_proj_kernel, q_scale=q_scale),
        grid=(m // tm,), in_specs=in_specs, out_specs=out_specs, out_shape=out_shape,
        compiler_params=_cparams(("parallel",)), name="even_proj")(*ins)


def _swa_bias(start, seq):
    tq = A_TQ
    span = tq + 2 * A_WINDOW
    t = start + lax.broadcasted_iota(I32, (tq, span), 0)
    src = start - A_WINDOW + lax.broadcasted_iota(I32, (tq, span), 1)
    dist = jnp.abs(t - src)
    valid = (dist <= A_WINDOW) & (src >= 0) & (src < seq)
    slopes = 2.0 ** (-8.0 * (jnp.arange(A_HEADS, dtype=F32) + 1.0) / A_HEADS) * LOG2E
    return jnp.where(valid[None], -slopes[:, None, None] * dist.astype(F32)[None], NEG_INF)


def _swa_kernel(sink_ref, q_ref, kp_ref, kc_ref, kn_ref, vp_ref, vc_ref, vn_ref, bias_ref, o_ref):
    tq = A_TQ
    kb = jnp.concatenate([kp_ref[0], kc_ref[0], kn_ref[0]], 0)
    vb = jnp.concatenate([vp_ref[0], vc_ref[0], vn_ref[0]], 0)
    lane = lax.broadcasted_iota(I32, (tq, PAIR), 1)
    outs = []
    for h in range(A_HEADS):
        g = h // A_REP
        q_pair = q_ref[0, :, (h // 2) * PAIR:(h // 2 + 1) * PAIR]
        qm = jnp.where((lane >= HEAD_DIM) == (h % 2 == 1), q_pair, jnp.zeros_like(q_pair))
        s = _dot_nt(qm, kb[:, g * PAIR:(g + 1) * PAIR]) + bias_ref[0, h]
        sink = sink_ref[h]
        m = jnp.maximum(jnp.max(s, -1, keepdims=True), sink)
        p = jnp.exp2(s - m)
        denom = jnp.sum(p, -1, keepdims=True) + jnp.exp2(sink - m)
        outs.append(_dot(p.astype(BF16), vb[:, g * PAIR:(g + 1) * PAIR]) / denom)
    for jp in range(A_HEADS // 2):
        o_ref[0, :, jp * PAIR:(jp + 1) * PAIR] = _own_half(outs[2 * jp], outs[2 * jp + 1]).astype(BF16)


def _swa(qa3, ka3, va3, a_sink):
    bsz, seq, _ = qa3.shape
    tq, w = A_TQ, A_WINDOW
    nblk = seq // w
    r = tq // w
    kw = 2 * A_KV
    prev = pl.BlockSpec((1, w, kw), lambda b, i: (b, jnp.maximum(i * r - 1, 0), 0))
    cur = pl.BlockSpec((1, tq, kw), lambda b, i: (b, i, 0))
    nxt = pl.BlockSpec((1, w, kw), lambda b, i: (b, jnp.minimum(i * r + r, nblk - 1), 0))
    qblk = pl.BlockSpec((1, tq, A_Q), lambda b, i: (b, i, 0))
    nstep = seq // tq
    assert nstep >= 3
    bias = jnp.stack([_swa_bias(s0, seq) for s0 in (0, tq, seq - tq)])
    bblk = pl.BlockSpec((1, A_HEADS, tq, tq + 2 * w),
                        lambda b, i: ((i > 0).astype(I32) + (i == nstep - 1).astype(I32), 0, 0, 0))
    return pl.pallas_call(
        _swa_kernel, grid=(bsz, nstep),
        in_specs=[pl.BlockSpec(memory_space=pltpu.SMEM), qblk, prev, cur, nxt, prev, cur, nxt, bblk],
        out_specs=qblk, out_shape=jax.ShapeDtypeStruct((bsz, seq, A_Q), BF16),
        compiler_params=_cparams(("parallel", "arbitrary")), name="swa_attn")(
            a_sink.astype(F32) * LOG2E, qa3, ka3, ka3, ka3, va3, va3, va3, bias)


def _mla_kernel(qt_ref, k_ref, vt_ref, o_ref, sa_ref, sb_ref, cma_ref, cmb_ref, m_ref, l_ref, acc_ref,
                *, tk, n_chunks):
    m_ref[...] = jnp.full_like(m_ref, NEG_INF)
    l_ref[...] = jnp.zeros_like(l_ref)
    acc_ref[...] = jnp.zeros_like(acc_ref)

    def scores(c, s_ref, cm_ref):
        rows = pl.ds(pl.multiple_of(c * tk, tk), tk)
        for h in range(B_HEADS):
            s = _dot(k_ref[0, rows, h * B_PAD:(h + 1) * B_PAD], qt_ref[0, h])
            s_ref[h] = s
            cm_ref[h] = jnp.max(s, 0, keepdims=True)

    def update(c, s_ref, cm_ref):
        for h in range(B_HEADS):
            m_old = m_ref[h]
            m_new = jnp.maximum(m_old, cm_ref[h])
            alpha = jnp.exp2(m_old - m_new)
            p = jnp.exp2(s_ref[h] - m_new)
            l_ref[h] = l_ref[h] * alpha + jnp.sum(p, 0, keepdims=True)
            acc_ref[h] = acc_ref[h] * alpha + _dot(vt_ref[0, h, c], p.astype(BF16))
            m_ref[h] = m_new

    a, b = (sa_ref, cma_ref), (sb_ref, cmb_ref)
    scores(0, *a)

    def body(i, _):
        scores(2 * i + 1, *b)
        update(2 * i, *a)
        scores(2 * i + 2, *a)
        update(2 * i + 1, *b)
        return 0

    lax.fori_loop(0, n_chunks // 2 - 1, body, 0)
    scores(n_chunks - 1, *b)
    update(n_chunks - 2, *a)
    update(n_chunks - 1, *b)
    for h in range(B_HEADS):
        o_ref[0, h] = (acc_ref[h] / l_ref[h]).astype(BF16)


def _mla(qt, k3, vt):
    bsz, _, _, seq = qt.shape
    tq = B_TQ
    nck, tk = vt.shape[2], vt.shape[4]
    return pl.pallas_call(
        functools.partial(_mla_kernel, tk=tk, n_chunks=nck),
        grid=(bsz, seq // tq),
        in_specs=[pl.BlockSpec((1, B_HEADS, B_PAD, tq), lambda b, i: (b, 0, 0, i)),
                  pl.BlockSpec((1, seq, B_HEADS * B_PAD), lambda b, i: (b, 0, 0)),
                  pl.BlockSpec((1, B_HEADS, nck, B_V, tk), lambda b, i: (b, 0, 0, 0, 0))],
        out_specs=pl.BlockSpec((1, B_HEADS, B_V, tq), lambda b, i: (b, 0, 0, i)),
        out_shape=jax.ShapeDtypeStruct((bsz, B_HEADS, B_V, seq), BF16),
        scratch_shapes=[pltpu.VMEM((B_HEADS, tk, tq), F32), pltpu.VMEM((B_HEADS, tk, tq), F32),
                        pltpu.VMEM((B_HEADS, 1, tq), F32), pltpu.VMEM((B_HEADS, 1, tq), F32),
                        pltpu.VMEM((B_HEADS, 1, tq), F32), pltpu.VMEM((B_HEADS, 1, tq), F32),
                        pltpu.VMEM((B_HEADS, B_V, tq), F32)],
        compiler_params=_cparams(("parallel", "arbitrary")), name="mla_attn")(qt, k3, vt)


def _outproj_kernel(*refs, n_row, n_t):
    n_in = n_row + n_t
    a_refs = refs[:n_in]
    w_refs = refs[n_in:2 * n_in]
    h_ref, g_ref, b_ref, wr_ref = refs[2 * n_in:2 * n_in + 4]
    hn_ref, hb_ref, lg_ref = refs[2 * n_in + 4:]
    mix = None
    for a_ref, w_ref in zip(a_refs[:n_row], w_refs[:n_row]):
        part = _dot(a_ref[...], w_ref[...])
        mix = part if mix is None else mix + part
    for a_ref, w_ref in zip(a_refs[n_row:], w_refs[n_row:]):
        at = a_ref[0]
        part = _dot_tn(at.reshape(at.shape[0] * at.shape[1], at.shape[2]), w_ref[...])
        mix = part if mix is None else mix + part
    hn = _layer_norm(DN_ALPHA * h_ref[...] + mix, g_ref[...], b_ref[...])
    hn_ref[...] = hn
    hb_ref[...] = hn.astype(BF16)
    hi_f = _bf16_part(hn)
    whl = wr_ref[...]
    parts = _dot_nt(whl, hi_f.astype(BF16)) + _dot_nt(whl, (hn - hi_f).astype(BF16))
    lg_ref[0] = parts[:N_EXPERTS] + parts[N_EXPERTS:]


def _outproj(rows, rows_w, trans, trans_w, h2, g, b, w_router, bsz, seq):
    m = h2.shape[0]
    tm = ROW_TILE
    nseq = seq // tm
    w_list = [w.astype(BF16) for w in list(rows_w) + list(trans_w)]
    wr_f = w_router.T.astype(F32)
    wr_hi = lax.bitcast_convert_type(lax.bitcast_convert_type(wr_f, jnp.uint32) & jnp.uint32(0xFFFF0000), F32)
    wr_t = jnp.concatenate([wr_hi, wr_f - wr_hi], 0).astype(BF16)
    row = lambda w: pl.BlockSpec((tm, w), lambda i: (i, 0))
    full = lambda a: pl.BlockSpec(a.shape, lambda i: (0,) * a.ndim)
    tblk = lambda a: pl.BlockSpec((1, a.shape[1], a.shape[2], tm), lambda i: (i // nseq, 0, 0, i % nseq))
    g2, b2 = g.reshape(1, -1).astype(F32), b.reshape(1, -1).astype(F32)
    ins = list(rows) + list(trans) + w_list + [h2, g2, b2, wr_t]
    in_specs = ([row(a.shape[1]) for a in rows] + [tblk(a) for a in trans] + [full(w) for w in w_list]
                + [row(D_MODEL), full(g2), full(b2), full(wr_t)])
    out_shape = [jax.ShapeDtypeStruct((m, D_MODEL), F32), jax.ShapeDtypeStruct((m, D_MODEL), BF16),
                 jax.ShapeDtypeStruct((bsz, N_EXPERTS, seq), F32)]
    out_specs = [row(D_MODEL), row(D_MODEL),
                 pl.BlockSpec((1, N_EXPERTS, tm), lambda i: (i // nseq, 0, i % nseq))]
    return pl.pallas_call(
        functools.partial(_outproj_kernel, n_row=len(rows), n_t=len(trans)), grid=(m // tm,),
        in_specs=in_specs, out_specs=out_specs, out_shape=out_shape,
        compiler_params=_cparams(("parallel",)), name="outproj_ln")(*ins)


def _route_kernel(lg_ref, pos_ref, gate_ref, *, cap, seq):
    lg = lg_ref[0]
    mx = jnp.max(lg, 0, keepdims=True)
    ex = jnp.exp(lg - mx)
    aff = ex / jnp.sum(ex, 0, keepdims=True)
    gate_ref[0] = aff

    def bis(_, carry):
        lo, hi = carry
        mid = lo + ((hi - lo + 1) >> 1)
        cnt = jnp.sum((aff >= pltpu.bitcast(mid, F32)).astype(I32), 1, keepdims=True)
        ok = cnt >= cap
        return jnp.where(ok, mid, lo), jnp.where(ok, hi, mid - 1)

    lo0 = jnp.zeros((N_EXPERTS, 1), I32)
    hi0 = jnp.full((N_EXPERTS, 1), 0x7F800000, I32)
    thr, _ = lax.fori_loop(0, 32, bis, (lo0, hi0))
    gt = aff >= pltpu.bitcast(thr + 1, F32)
    eq = (aff >= pltpu.bitcast(thr, F32)) & jnp.logical_not(gt)
    need = cap - jnp.sum(gt.astype(I32), 1, keepdims=True)

    ck = TOK_CHUNK
    nck = seq // ck
    tri = (lax.broadcasted_iota(I32, (ck, ck), 0) <= lax.broadcasted_iota(I32, (ck, ck), 1)).astype(BF16)

    def chunk_cumsum(mask, k, run):
        mk = mask[:, k * ck:(k + 1) * ck]
        inc = _dot(mk.astype(BF16), tri).astype(I32)
        return mk, inc + run, run + inc[:, ck - 1:ck]

    run = jnp.zeros((N_EXPERTS, 1), I32)
    sel_chunks = []
    for k in range(nck):
        mk, inc, run = chunk_cumsum(eq, k, run)
        sel_chunks.append(gt[:, k * ck:(k + 1) * ck] | (mk & (inc <= need)))

    run = jnp.zeros((N_EXPERTS, 1), I32)
    for k in range(nck):
        mk = sel_chunks[k]
        inc = _dot(mk.astype(BF16), tri).astype(I32) + run
        pos_ref[0, :, k * ck:(k + 1) * ck] = jnp.where(mk, inc - 1, -1)
        run = inc[:, ck - 1:ck]


def _route(logits_t, cap):
    bsz, _, seq = logits_t.shape
    blk = pl.BlockSpec((1, N_EXPERTS, seq), lambda b: (b, 0, 0))
    return pl.pallas_call(
        functools.partial(_route_kernel, cap=cap, seq=seq), grid=(bsz,),
        in_specs=[blk], out_specs=[blk, blk],
        out_shape=[jax.ShapeDtypeStruct((bsz, N_EXPERTS, seq), I32),
                   jax.ShapeDtypeStruct((bsz, N_EXPERTS, seq), F32)],
        compiler_params=_cparams(("parallel",)), name="route_topk")(logits_t)


def _ffn_kernel(pos_ref, gate_ref, h_ref, wg_ref, wu_ref, wd_ref, y_ref, x_ref, gs_ref, acc_ref, *, cap, nck):
    fc = pl.program_id(2)

    @pl.when(fc == 0)
    def _():
        acc_ref[...] = jnp.zeros_like(acc_ref)
        gs_ref[...] = jnp.zeros_like(gs_ref)
        slot = lax.broadcasted_iota(I32, (cap, GATHER_CHUNK), 0)

        def gather(k, _):
            match = slot == pos_ref[0, 0, pl.ds(k, 1), :]
            rows = pl.ds(pl.multiple_of(k * GATHER_CHUNK, GATHER_CHUNK), GATHER_CHUNK)
            acc_ref[...] += _dot(match.astype(BF16), h_ref[0, rows, :])
            gs_ref[...] += jnp.sum(jnp.where(match, gate_ref[0, 0, pl.ds(k, 1), :], 0.0), 1, keepdims=True)
            return 0

        lax.fori_loop(0, nck, gather, 0)
        x_ref[...] = acc_ref[...].astype(BF16)
        acc_ref[...] = jnp.zeros_like(acc_ref)

    x = x_ref[...]
    g = _dot(x, wg_ref[0].astype(BF16))
    u = _dot(x, wu_ref[0].astype(BF16))
    hid = (g * jax.nn.sigmoid(g) * u).astype(BF16)
    acc_ref[...] += _dot(hid, wd_ref[0].astype(BF16))

    @pl.when(fc == pl.num_programs(2) - 1)
    def _():
        y_ref[0, 0] = (acc_ref[...] * gs_ref[...]).astype(BF16)


def _ffn(hb3, pos, gate, w_gate, w_up, w_down, cap):
    bsz, seq, d = hb3.shape
    ne, f = w_gate.shape[0], w_gate.shape[-1]
    nck = seq // GATHER_CHUNK
    fck = FFN_CHUNK
    pos4 = pos.reshape(bsz, ne, nck, GATHER_CHUNK)
    gate4 = gate.reshape(bsz, ne, nck, GATHER_CHUNK)
    tok = pl.BlockSpec((1, 1, nck, GATHER_CHUNK), lambda b, e, c: (b, e, 0, 0))
    return pl.pallas_call(
        functools.partial(_ffn_kernel, cap=cap, nck=nck), grid=(bsz, ne, f // fck),
        in_specs=[tok, tok,
                  pl.BlockSpec((1, seq, d), lambda b, e, c: (b, 0, 0)),
                  pl.BlockSpec((1, d, fck), lambda b, e, c: (e, 0, c)),
                  pl.BlockSpec((1, d, fck), lambda b, e, c: (e, 0, c)),
                  pl.BlockSpec((1, fck, d), lambda b, e, c: (e, c, 0))],
        out_specs=pl.BlockSpec((1, 1, cap, d), lambda b, e, c: (b, e, 0, 0)),
        out_shape=jax.ShapeDtypeStruct((bsz, ne, cap, d), BF16),
        scratch_shapes=[pltpu.VMEM((cap, d), BF16), pltpu.VMEM((cap, 1), F32), pltpu.VMEM((cap, d), F32)],
        compiler_params=_cparams(("parallel", "arbitrary", "arbitrary")), name="moe_ffn")(
            pos4, gate4, hb3, w_gate, w_up, w_down)


def _combine_kernel(pos_ref, y_ref, h_ref, g_ref, b_ref, hn_ref, hb_ref, *, cap, group):
    tt = h_ref.shape[1]
    ne = y_ref.shape[1]
    d = y_ref.shape[-1]
    slot = lax.broadcasted_iota(I32, (group, cap, tt), 1)
    acc = jnp.zeros((tt, d), F32)
    for e0 in range(0, ne, group):
        onehot = (slot == pos_ref[0, e0:e0 + group]).astype(BF16).reshape(group * cap, tt)
        ys = y_ref[0, e0:e0 + group].reshape(group * cap, d)
        acc = acc + _dot_tn(onehot, ys)
    hn = _layer_norm(DN_ALPHA * h_ref[0] + acc, g_ref[...], b_ref[...])
    hn_ref[0] = hn
    hb_ref[0] = hn.astype(BF16)


def _combine(y, pos, h3, g, b):
    bsz, ne, cap, d = y.shape
    seq = h3.shape[1]
    tt = COMBINE_TOKENS
    pos4 = pos.reshape(bsz, ne, 1, seq)
    g2, b2 = g.reshape(1, -1).astype(F32), b.reshape(1, -1).astype(F32)
    hblk = pl.BlockSpec((1, tt, d), lambda bb, k: (bb, k, 0))
    vec = pl.BlockSpec((1, d), lambda bb, k: (0, 0))
    return pl.pallas_call(
        functools.partial(_combine_kernel, cap=cap, group=4), grid=(bsz, seq // tt),
        in_specs=[pl.BlockSpec((1, ne, 1, tt), lambda bb, k: (bb, 0, 0, k)),
                  pl.BlockSpec((1, ne, cap, d), lambda bb, k: (bb, 0, 0, 0), pipeline_mode=pl.Buffered(1)),
                  hblk, vec, vec],
        out_specs=[hblk, hblk],
        out_shape=[jax.ShapeDtypeStruct(h3.shape, F32), jax.ShapeDtypeStruct(h3.shape, BF16)],
        compiler_params=_cparams(("parallel", "arbitrary")), name="moe_combine_ln")(pos4, y, h3, g2, b2)


def _moe_block(h2, hb2, logits_t, w_gate, w_up, w_down, g, b, bsz, seq):
    cap = 2 * seq // N_EXPERTS
    pos, gate = _route(logits_t, cap)
    y = _ffn(hb2.reshape(bsz, seq, D_MODEL), pos, gate, w_gate, w_up, w_down, cap)
    hn, hnb = _combine(y, pos, h2.reshape(bsz, seq, D_MODEL), g, b)
    return hn.reshape(bsz * seq, D_MODEL), hnb.reshape(bsz * seq, D_MODEL)


def _qkv_kernel(x_ref, wqt_ref, wkv_ref, qt_ref, k_ref, v_ref):
    x = x_ref[...]
    w = C_HEADS * HEAD_DIM
    qt_ref[0] = (_dot_nt(wqt_ref[...], x) * (HEAD_DIM ** -0.5 * LOG2E)).astype(BF16)
    k_ref[...] = _dot(x, wkv_ref[:, :w]).astype(BF16)
    v_ref[...] = _dot(x, wkv_ref[:, w:]).astype(BF16)


def _qkv(hb2, w_qkv, bsz, seq):
    m = hb2.shape[0]
    tm = ROW_TILE
    nseq = seq // tm
    w = C_HEADS * HEAD_DIM
    wqt = w_qkv[:, :w].T.astype(BF16)
    wkv = w_qkv[:, w:].astype(BF16)
    row = pl.BlockSpec((tm, w), lambda i: (i, 0))
    return pl.pallas_call(
        _qkv_kernel, grid=(m // tm,),
        in_specs=[pl.BlockSpec((tm, D_MODEL), lambda i: (i, 0)),
                  pl.BlockSpec((w, D_MODEL), lambda i: (0, 0)),
                  pl.BlockSpec((D_MODEL, 2 * w), lambda i: (0, 0))],
        out_specs=[pl.BlockSpec((1, w, tm), lambda i: (i // nseq, 0, i % nseq)), row, row],
        out_shape=[jax.ShapeDtypeStruct((bsz, w, seq), BF16),
                   jax.ShapeDtypeStruct((m, w), BF16), jax.ShapeDtypeStruct((m, w), BF16)],
        compiler_params=_cparams(("parallel",)), name="qkv_proj")(hb2, wqt, wkv)


def _na_window_mask(r0, rows):
    nq, nk = C_QROWS * GRID_W, C_KROWS * GRID_W
    kr0 = jnp.clip(r0 - C_WIN_H // 2, 0, rows - C_KROWS)
    ki = lax.broadcasted_iota(I32, (nk, nq), 0)
    qi = lax.broadcasted_iota(I32, (nk, nq), 1)
    qrow, qcol = r0 + qi // GRID_W, qi % GRID_W
    krow, kcol = kr0 + ki // GRID_W, ki % GRID_W
    rs = jnp.clip(qrow - C_WIN_H // 2, 0, rows - C_WIN_H)
    cs = jnp.clip(qcol - C_WIN_W // 2, 0, GRID_W - C_WIN_W)
    valid = (krow >= rs) & (krow < rs + C_WIN_H) & (kcol >= cs) & (kcol < cs + C_WIN_W)
    return jnp.where(valid, 0.0, NEG_INF).astype(F32)


def _na_kernel(qt_ref, k_ref, v_ref, t2_ref, mask_ref, o_ref, s_ref, *, rows):
    rb = pl.program_id(2)
    nq = C_QROWS * GRID_W
    nk = C_KROWS * GRID_W
    r0 = rb * C_QROWS
    kr0 = jnp.clip(r0 - C_WIN_H // 2, 0, rows - C_KROWS)
    kstart = pl.multiple_of(kr0 * GRID_W, GRID_W)
    mask_add = mask_ref[0]
    dbase = kr0 - r0 + C_WIN_H - 1
    feat = lax.broadcasted_iota(I32, (PAIR, nq), 0)
    for jp in range(C_PAIRS):
        psl = slice(jp * PAIR, (jp + 1) * PAIR)
        qt_pair = qt_ref[0, psl, :]
        k_pair = k_ref[0, pl.ds(kstart, nk), psl]
        for u in range(2):
            qm = jnp.where((feat >= HEAD_DIM) == (u == 1), qt_pair, jnp.zeros_like(qt_pair))
            s_ref[2 * jp + u] = _dot(k_pair, qm)
    for jp in range(C_PAIRS):
        psl = slice(jp * PAIR, (jp + 1) * PAIR)
        v_pair = v_ref[0, pl.ds(kstart, nk), psl]
        outs = []
        for u in range(2):
            h = 2 * jp + u
            bias_rows = []
            for i in range(C_KROWS):
                tiles = [t2_ref[h, jnp.clip(dbase + i - 2 * jq, 0, 2 * C_WIN_H - 1)] for jq in range(C_QROWS // 2)]
                bias_rows.append(jnp.concatenate(tiles, 1))
            s = s_ref[h] + jnp.concatenate(bias_rows, 0) + mask_add
            m = jnp.max(s, 0, keepdims=True)
            p = jnp.exp2(s - m)
            l = jnp.sum(p, 0, keepdims=True)
            outs.append(_dot_tn(v_pair, p.astype(BF16)) / l)
        o_ref[0, psl, :] = jnp.where(feat < HEAD_DIM, outs[0], outs[1]).astype(BF16)


def _natten(qt3, k3, v3, rpb):
    bsz, seq, _ = k3.shape
    rows = seq // GRID_W
    hps = 2 * C_PAIRS
    cols = jnp.arange(GRID_W)
    dc = jnp.clip(cols[:, None] - cols[None, :] + C_WIN_W - 1, 0, 2 * C_WIN_W - 2)
    tt = (rpb.astype(F32) * LOG2E)[:, :, dc]
    zero = jnp.zeros_like(tt[:, :1])
    text = jnp.concatenate([zero, tt, zero], 1)
    t2 = jnp.concatenate([text[:, 1:], text[:, :-1]], -1)
    nd = t2.shape[1]
    nq = C_QROWS * GRID_W
    nk = C_KROWS * GRID_W
    nrb = rows // C_QROWS
    assert nrb >= 3
    masks = jnp.stack([_na_window_mask(r0, rows) for r0 in (0, C_QROWS, rows - C_QROWS)])
    w = C_PAIRS * PAIR
    qblk = pl.BlockSpec((1, w, nq), lambda b, g, r: (b, g, r))
    kblk = pl.BlockSpec((1, seq, w), lambda b, g, r: (b, 0, g))
    return pl.pallas_call(
        functools.partial(_na_kernel, rows=rows),
        grid=(bsz, C_HEADS // hps, nrb),
        in_specs=[qblk, kblk, kblk,
                  pl.BlockSpec((hps, nd, GRID_W, 2 * GRID_W), lambda b, g, r: (g, 0, 0, 0)),
                  pl.BlockSpec((1, nk, nq), lambda b, g, r: ((r > 0).astype(I32) + (r == nrb - 1).astype(I32), 0, 0))],
        out_specs=qblk, out_shape=jax.ShapeDtypeStruct((bsz, C_HEADS * HEAD_DIM, seq), BF16),
        scratch_shapes=[pltpu.VMEM((hps, C_KROWS * GRID_W, nq), F32)],
        compiler_params=_cparams(("parallel", "parallel", "arbitrary")), name="natten")(qt3, k3, v3, t2, masks)


def kernel(x, w_in0, a_sink, mla_q_norm, w_q_up, mla_kv_norm, w_kv_up, w_out0, ln0a_g, ln0a_b,
           router0, w_gate0, w_up0, w_down0, ln0b_g, ln0b_b,
           w_qkv1, na_rpb, w_out1, ln1a_g, ln1a_b,
           router1, w_gate1, w_up1, w_down1, ln1b_g, ln1b_b):
    bsz, seq, d = x.shape
    m = bsz * seq
    x2 = x.reshape(m, d)
    qa, ka, va, qt, km, vt = _even_proj(x2, w_in0, mla_q_norm, w_q_up, mla_kv_norm, w_kv_up, bsz, seq)
    out_a = _swa(qa.reshape(bsz, seq, -1), ka.reshape(bsz, seq, -1), va.reshape(bsz, seq, -1), a_sink)
    out_bt = _mla(qt, km.reshape(bsz, seq, -1), vt)
    h, hb, lg = _outproj([out_a.reshape(m, A_Q)], [w_out0[:A_Q]], [out_bt], [w_out0[A_Q:]],
                         x2, ln0a_g, ln0a_b, router0, bsz, seq)
    h, hb = _moe_block(h, hb, lg, w_gate0, w_up0, w_down0, ln0b_g, ln0b_b, bsz, seq)
    qt1, k2, v2 = _qkv(hb, w_qkv1, bsz, seq)
    na_t = _natten(qt1, k2.reshape(bsz, seq, -1), v2.reshape(bsz, seq, -1), na_rpb)
    h, hb, lg = _outproj([], [], [na_t.reshape(bsz, C_HEADS, HEAD_DIM, seq)], [w_out1],
                         h, ln1a_g, ln1a_b, router1, bsz, seq)
    h, _ = _moe_block(h, hb, lg, w_gate1, w_up1, w_down1, ln1b_g, ln1b_b, bsz, seq)
    return h.reshape(bsz, seq, d)
```

```python
import functools
import math

import jax
import jax.numpy as jnp
from jax import lax
from jax.experimental import pallas as pl
from jax.experimental.pallas import tpu as pltpu

F32 = jnp.float32
BF16 = jnp.bfloat16
I32 = jnp.int32

D_MODEL = 1024
DEPTH = 2
HEAD_DIM = 64
NEG_INF = -1e30
A_HEADS = 8
A_KV_HEADS = 2
A_REP = A_HEADS // A_KV_HEADS
A_WINDOW = 128
A_Q = A_HEADS * HEAD_DIM
A_KV = A_KV_HEADS * HEAD_DIM
A_TQ = 256
B_HEADS = 8
B_Q_RANK = 384
B_KV_RANK = 256
B_NOPE = 64
B_ROPE = 32
B_V = 64
B_PAD = 128
B_TQ = 512
ROPE_BASE = 10000.0
C_HEADS = 16
GRID_W = 64
C_WIN_H = 8
C_WIN_W = 16
C_QROWS = 4
C_KROWS = 12
C_PAIRS = 4
N_EXPERTS = 16
D_EXPERT = 2 * D_MODEL
TOK_CHUNK = 256
GATHER_CHUNK = 1024
COMBINE_TOKENS = 1024
FFN_CHUNK = 1024
LN_EPS = 1e-5
RMS_EPS = 1e-6
DN_ALPHA = (2 * DEPTH) ** 0.25

LOG2E = math.log2(math.e)
LANES = 128
PAIR = 2 * HEAD_DIM
VMEM_LIMIT = 56 * 1024 * 1024
ROW_TILE = 512


def _cparams(sem):
    return pltpu.CompilerParams(dimension_semantics=sem, vmem_limit_bytes=VMEM_LIMIT)


def _layer_norm(z, g, b):
    mu = jnp.mean(z, -1, keepdims=True)
    zc = z - mu
    var = jnp.mean(zc * zc, -1, keepdims=True)
    return zc * lax.rsqrt(var + LN_EPS) * g + b


def _rms_norm(c, g):
    return c * lax.rsqrt(jnp.mean(c * c, -1, keepdims=True) + RMS_EPS) * g


def _bf16_part(x):
    return pltpu.bitcast(pltpu.bitcast(x, I32) & jnp.int32(-65536), F32)


def _dot(a, b):
    return jnp.dot(a, b, preferred_element_type=F32)


def _dot_nt(a, b):
    return lax.dot_general(a, b, (((1,), (1,)), ((), ())), preferred_element_type=F32)


def _dot_tn(a, b):
    return lax.dot_general(a, b, (((0,), (0,)), ((), ())), preferred_element_type=F32)


def _own_half(lo, hi):
    lane = lax.broadcasted_iota(I32, lo.shape, 1)
    return jnp.where(lane < HEAD_DIM, lo, hi)


def _even_proj_kernel(x_ref, wa_ref, wcq_ref, wckv_ref, wkpr_ref, gq_ref, gkv_ref,
                      wq1t_ref, wkn_ref, wvt_ref, cos_ref, sin_ref, cost_ref, sint_ref,
                      qa_ref, ka_ref, va_ref, qt_ref, k_ref, vt_ref, *, q_scale):
    xb = x_ref[...].astype(BF16)
    a = _dot(xb, wa_ref[...])
    qa_ref[...] = (a[:, :A_Q] * (HEAD_DIM ** -0.5 * LOG2E)).astype(BF16)
    ka_ref[...] = a[:, A_Q:A_Q + 2 * A_KV].astype(BF16)
    va_ref[...] = a[:, A_Q + 2 * A_KV:].astype(BF16)

    nq = _rms_norm(_dot(xb, wcq_ref[...]), gq_ref[...]).astype(BF16)
    qpt = _dot_nt(wq1t_ref[...], nq)
    cost, sint = cost_ref[...], sint_ref[...]
    half = B_ROPE // 2
    tm = qpt.shape[-1]
    for h in range(B_HEADS):
        blk = qpt[h * B_PAD:(h + 1) * B_PAD]
        rot = jnp.concatenate([jnp.zeros((B_NOPE, tm), F32), -blk[B_NOPE + half:B_NOPE + B_ROPE],
                               blk[B_NOPE:B_NOPE + half], jnp.zeros((B_PAD - B_NOPE - B_ROPE, tm), F32)], 0)
        qt_ref[0, h] = ((blk * cost + rot * sint) * q_scale).astype(BF16)

    nkv = _rms_norm(_dot(xb, wckv_ref[...]), gkv_ref[...]).astype(BF16)
    kn = _dot(nkv, wkn_ref[...])
    kpr = _dot(xb, wkpr_ref[...])
    kpe = kpr[:, :B_PAD] * cos_ref[...] + kpr[:, B_PAD:] * sin_ref[...]
    for h in range(B_HEADS):
        sl = slice(h * B_PAD, (h + 1) * B_PAD)
        k_ref[:, sl] = (kn[:, sl] + kpe).astype(BF16)
    vt = _dot_nt(wvt_ref[...], nkv)
    vt_ref[0, :, 0] = vt.reshape(B_HEADS, B_V, vt.shape[-1]).astype(BF16)


def _even_proj(x2, w_in, gq, w_q_up, gkv, w_kv_up, bsz, seq):
    m = x2.shape[0]
    tm = ROW_TILE
    nseq = seq // tm
    o1, o2, o3 = A_Q, A_Q + A_KV, A_Q + 2 * A_KV
    o4, o5 = o3 + B_Q_RANK, o3 + B_Q_RANK + B_KV_RANK
    half = B_ROPE // 2
    wk = w_in[:, o1:o2].reshape(D_MODEL, A_KV_HEADS, 1, HEAD_DIM)
    wv = w_in[:, o2:o3].reshape(D_MODEL, A_KV_HEADS, 1, HEAD_DIM)
    dup = lambda w: jnp.broadcast_to(w, (D_MODEL, A_KV_HEADS, 2, HEAD_DIM)).reshape(D_MODEL, 2 * A_KV)
    wa = jnp.concatenate([w_in[:, :o1], dup(wk), dup(wv)], 1).astype(BF16)
    wcq = w_in[:, o3:o4].astype(BF16)
    wckv = w_in[:, o4:o5].astype(BF16)
    wkr_raw = w_in[:, o5:]
    zpad = jnp.zeros((D_MODEL, B_PAD - B_NOPE - B_ROPE), F32)
    znope = jnp.zeros((D_MODEL, B_NOPE), F32)
    wkpr = jnp.concatenate([znope, wkr_raw, zpad,
                            znope, -wkr_raw[:, half:], wkr_raw[:, :half], zpad], 1).astype(BF16)
    wq = w_q_up.reshape(B_Q_RANK, B_HEADS, B_NOPE + B_ROPE)
    zq = jnp.zeros((B_Q_RANK, B_HEADS, B_PAD - B_NOPE - B_ROPE), F32)
    wq1t = jnp.concatenate([wq, zq], -1).reshape(B_Q_RANK, B_HEADS * B_PAD).T.astype(BF16)
    wkv = w_kv_up.reshape(B_KV_RANK, B_HEADS, B_NOPE + B_V)
    zk = jnp.zeros((B_KV_RANK, B_HEADS, B_PAD - B_NOPE), F32)
    wkn = jnp.concatenate([wkv[..., :B_NOPE], zk], -1).reshape(B_KV_RANK, B_HEADS * B_PAD).astype(BF16)
    wvt = wkv[..., B_NOPE:].reshape(B_KV_RANK, B_HEADS * B_V).T.astype(BF16)
    inv = ROPE_BASE ** (-jnp.arange(half, dtype=F32) / half)
    inv_l = jnp.concatenate([jnp.zeros((B_NOPE,), F32), inv, inv, jnp.zeros((B_PAD - B_NOPE - B_ROPE,), F32)])
    ang = jnp.arange(seq, dtype=I32).astype(F32)[:, None] * inv_l[None, :]
    cos_t, sin_t = jnp.cos(ang), jnp.sin(ang)

    full = lambda a: pl.BlockSpec(a.shape, lambda i: (0,) * a.ndim)
    row = lambda w: pl.BlockSpec((tm, w), lambda i: (i, 0))
    tab = pl.BlockSpec((tm, B_PAD), lambda i: (i % nseq, 0))
    tab_t = pl.BlockSpec((B_PAD, tm), lambda i: (0, i % nseq))
    gq2, gkv2 = gq.reshape(1, -1).astype(F32), gkv.reshape(1, -1).astype(F32)
    ins = [x2, wa, wcq, wckv, wkpr, gq2, gkv2, wq1t, wkn, wvt, cos_t, sin_t, cos_t.T, sin_t.T]
    in_specs = [row(D_MODEL)] + [full(a) for a in ins[1:10]] + [tab, tab, tab_t, tab_t]
    out_shape = [jax.ShapeDtypeStruct((m, A_Q), BF16),
                 jax.ShapeDtypeStruct((m, 2 * A_KV), BF16), jax.ShapeDtypeStruct((m, 2 * A_KV), BF16),
                 jax.ShapeDtypeStruct((bsz, B_HEADS, B_PAD, seq), BF16),
                 jax.ShapeDtypeStruct((m, B_HEADS * B_PAD), BF16),
                 jax.ShapeDtypeStruct((bsz, B_HEADS, nseq, B_V, tm), BF16)]
    out_specs = [row(A_Q), row(2 * A_KV), row(2 * A_KV),
                 pl.BlockSpec((1, B_HEADS, B_PAD, tm), lambda i: (i // nseq, 0, 0, i % nseq)),
                 row(B_HEADS * B_PAD),
                 pl.BlockSpec((1, B_HEADS, 1, B_V, tm), lambda i: (i // nseq, 0, i % nseq, 0, 0))]
    q_scale = (B_NOPE + B_ROPE) ** -0.5 * LOG2E
    return pl.pallas_call(
        functools.partial(_even_proj_kernel, q_scale=q_scale),
        grid=(m // tm,), in_specs=in_specs, out_specs=out_specs, out_shape=out_shape,
        compiler_params=_cparams(("parallel",)), name="even_proj")(*ins)


def _swa_bias(start, seq):
    tq = A_TQ
    span = tq + 2 * A_WINDOW
    t = start + lax.broadcasted_iota(I32, (tq, span), 0)
    src = start - A_WINDOW + lax.broadcasted_iota(I32, (tq, span), 1)
    dist = jnp.abs(t - src)
    valid = (dist <= A_WINDOW) & (src >= 0) & (src < seq)
    slopes = 2.0 ** (-8.0 * (jnp.arange(A_HEADS, dtype=F32) + 1.0) / A_HEADS) * LOG2E
    return jnp.where(valid[None], -slopes[:, None, None] * dist.astype(F32)[None], NEG_INF)


def _swa_kernel(sink_ref, q_ref, kp_ref, kc_ref, kn_ref, vp_ref, vc_ref, vn_ref, bias_ref, o_ref):
    tq = A_TQ
    kb = jnp.concatenate([kp_ref[0], kc_ref[0], kn_ref[0]], 0)
    vb = jnp.concatenate([vp_ref[0], vc_ref[0], vn_ref[0]], 0)
    lane = lax.broadcasted_iota(I32, (tq, PAIR), 1)
    outs = []
    for h in range(A_HEADS):
        g = h // A_REP
        q_pair = q_ref[0, :, (h // 2) * PAIR:(h // 2 + 1) * PAIR]
        qm = jnp.where((lane >= HEAD_DIM) == (h % 2 == 1), q_pair, jnp.zeros_like(q_pair))
        s = _dot_nt(qm, kb[:, g * PAIR:(g + 1) * PAIR]) + bias_ref[0, h]
        sink = sink_ref[h]
        m = jnp.maximum(jnp.max(s, -1, keepdims=True), sink)
        p = jnp.exp2(s - m)
        denom = jnp.sum(p, -1, keepdims=True) + jnp.exp2(sink - m)
        outs.append(_dot(p.astype(BF16), vb[:, g * PAIR:(g + 1) * PAIR]) / denom)
    for jp in range(A_HEADS // 2):
        o_ref[0, :, jp * PAIR:(jp + 1) * PAIR] = _own_half(outs[2 * jp], outs[2 * jp + 1]).astype(BF16)


def _swa(qa3, ka3, va3, a_sink):
    bsz, seq, _ = qa3.shape
    tq, w = A_TQ, A_WINDOW
    nblk = seq // w
    r = tq // w
    kw = 2 * A_KV
    prev = pl.BlockSpec((1, w, kw), lambda b, i: (b, jnp.maximum(i * r - 1, 0), 0))
    cur = pl.BlockSpec((1, tq, kw), lambda b, i: (b, i, 0))
    nxt = pl.BlockSpec((1, w, kw), lambda b, i: (b, jnp.minimum(i * r + r, nblk - 1), 0))
    qblk = pl.BlockSpec((1, tq, A_Q), lambda b, i: (b, i, 0))
    nstep = seq // tq
    assert nstep >= 3
    bias = jnp.stack([_swa_bias(s0, seq) for s0 in (0, tq, seq - tq)])
    bblk = pl.BlockSpec((1, A_HEADS, tq, tq + 2 * w),
                        lambda b, i: ((i > 0).astype(I32) + (i == nstep - 1).astype(I32), 0, 0, 0))
    return pl.pallas_call(
        _swa_kernel, grid=(bsz, nstep),
        in_specs=[pl.BlockSpec(memory_space=pltpu.SMEM), qblk, prev, cur, nxt, prev, cur, nxt, bblk],
        out_specs=qblk, out_shape=jax.ShapeDtypeStruct((bsz, seq, A_Q), BF16),
        compiler_params=_cparams(("parallel", "arbitrary")), name="swa_attn")(
            a_sink.astype(F32) * LOG2E, qa3, ka3, ka3, ka3, va3, va3, va3, bias)


def _mla_kernel(qt_ref, k_ref, vt_ref, o_ref, sa_ref, sb_ref, cma_ref, cmb_ref, m_ref, l_ref, acc_ref,
                *, tk, n_chunks):
    m_ref[...] = jnp.full_like(m_ref, NEG_INF)
    l_ref[...] = jnp.zeros_like(l_ref)
    acc_ref[...] = jnp.zeros_like(acc_ref)

    def scores(c, s_ref, cm_ref):
        rows = pl.ds(pl.multiple_of(c * tk, tk), tk)
        for h in range(B_HEADS):
            s = _dot(k_ref[0, rows, h * B_PAD:(h + 1) * B_PAD], qt_ref[0, h])
            s_ref[h] = s
            cm_ref[h] = jnp.max(s, 0, keepdims=True)

    def update(c, s_ref, cm_ref):
        for h in range(B_HEADS):
            m_old = m_ref[h]
            m_new = jnp.maximum(m_old, cm_ref[h])
            alpha = jnp.exp2(m_old - m_new)
            p = jnp.exp2(s_ref[h] - m_new)
            l_ref[h] = l_ref[h] * alpha + jnp.sum(p, 0, keepdims=True)
            acc_ref[h] = acc_ref[h] * alpha + _dot(vt_ref[0, h, c], p.astype(BF16))
            m_ref[h] = m_new

    a, b = (sa_ref, cma_ref), (sb_ref, cmb_ref)
    scores(0, *a)

    def body(i, _):
        scores(2 * i + 1, *b)
        update(2 * i, *a)
        scores(2 * i + 2, *a)
        update(2 * i + 1, *b)
        return 0

    lax.fori_loop(0, n_chunks // 2 - 1, body, 0)
    scores(n_chunks - 1, *b)
    update(n_chunks - 2, *a)
    update(n_chunks - 1, *b)
    for h in range(B_HEADS):
        o_ref[0, h] = (acc_ref[h] / l_ref[h]).astype(BF16)


def _mla(qt, k3, vt):
    bsz, _, _, seq = qt.shape
    tq = B_TQ
    nck, tk = vt.shape[2], vt.shape[4]
    return pl.pallas_call(
        functools.partial(_mla_kernel, tk=tk, n_chunks=nck),
        grid=(bsz, seq // tq),
        in_specs=[pl.BlockSpec((1, B_HEADS, B_PAD, tq), lambda b, i: (b, 0, 0, i)),
                  pl.BlockSpec((1, seq, B_HEADS * B_PAD), lambda b, i: (b, 0, 0)),
                  pl.BlockSpec((1, B_HEADS, nck, B_V, tk), lambda b, i: (b, 0, 0, 0, 0))],
        out_specs=pl.BlockSpec((1, B_HEADS, B_V, tq), lambda b, i: (b, 0, 0, i)),
        out_shape=jax.ShapeDtypeStruct((bsz, B_HEADS, B_V, seq), BF16),
        scratch_shapes=[pltpu.VMEM((B_HEADS, tk, tq), F32), pltpu.VMEM((B_HEADS, tk, tq), F32),
                        pltpu.VMEM((B_HEADS, 1, tq), F32), pltpu.VMEM((B_HEADS, 1, tq), F32),
                        pltpu.VMEM((B_HEADS, 1, tq), F32), pltpu.VMEM((B_HEADS, 1, tq), F32),
                        pltpu.VMEM((B_HEADS, B_V, tq), F32)],
        compiler_params=_cparams(("parallel", "arbitrary")), name="mla_attn")(qt, k3, vt)


def _outproj_kernel(*refs, n_row, n_t):
    n_in = n_row + n_t
    a_refs = refs[:n_in]
    w_refs = refs[n_in:2 * n_in]
    h_ref, g_ref, b_ref, wr_ref = refs[2 * n_in:2 * n_in + 4]
    hn_ref, hb_ref, lg_ref = refs[2 * n_in + 4:]
    mix = None
    for a_ref, w_ref in zip(a_refs[:n_row], w_refs[:n_row]):
        part = _dot(a_ref[...], w_ref[...])
        mix = part if mix is None else mix + part
    for a_ref, w_ref in zip(a_refs[n_row:], w_refs[n_row:]):
        at = a_ref[0]
        part = _dot_tn(at.reshape(at.shape[0] * at.shape[1], at.shape[2]), w_ref[...])
        mix = part if mix is None else mix + part
    hn = _layer_norm(DN_ALPHA * h_ref[...] + mix, g_ref[...], b_ref[...])
    hn_ref[...] = hn
    hb_ref[...] = hn.astype(BF16)
    hi_f = _bf16_part(hn)
    whl = wr_ref[...]
    parts = _dot_nt(whl, hi_f.astype(BF16)) + _dot_nt(whl, (hn - hi_f).astype(BF16))
    lg_ref[0] = parts[:N_EXPERTS] + parts[N_EXPERTS:]


def _outproj(rows, rows_w, trans, trans_w, h2, g, b, w_router, bsz, seq):
    m = h2.shape[0]
    tm = ROW_TILE
    nseq = seq // tm
    w_list = [w.astype(BF16) for w in list(rows_w) + list(trans_w)]
    wr_f = w_router.T.astype(F32)
    wr_hi = lax.bitcast_convert_type(lax.bitcast_convert_type(wr_f, jnp.uint32) & jnp.uint32(0xFFFF0000), F32)
    wr_t = jnp.concatenate([wr_hi, wr_f - wr_hi], 0).astype(BF16)
    row = lambda w: pl.BlockSpec((tm, w), lambda i: (i, 0))
    full = lambda a: pl.BlockSpec(a.shape, lambda i: (0,) * a.ndim)
    tblk = lambda a: pl.BlockSpec((1, a.shape[1], a.shape[2], tm), lambda i: (i // nseq, 0, 0, i % nseq))
    g2, b2 = g.reshape(1, -1).astype(F32), b.reshape(1, -1).astype(F32)
    ins = list(rows) + list(trans) + w_list + [h2, g2, b2, wr_t]
    in_specs = ([row(a.shape[1]) for a in rows] + [tblk(a) for a in trans] + [full(w) for w in w_list]
                + [row(D_MODEL), full(g2), full(b2), full(wr_t)])
    out_shape = [jax.ShapeDtypeStruct((m, D_MODEL), F32), jax.ShapeDtypeStruct((m, D_MODEL), BF16),
                 jax.ShapeDtypeStruct((bsz, N_EXPERTS, seq), F32)]
    out_specs = [row(D_MODEL), row(D_MODEL),
                 pl.BlockSpec((1, N_EXPERTS, tm), lambda i: (i // nseq, 0, i % nseq))]
    return pl.pallas_call(
        functools.partial(_outproj_kernel, n_row=len(rows), n_t=len(trans)), grid=(m // tm,),
        in_specs=in_specs, out_specs=out_specs, out_shape=out_shape,
        compiler_params=_cparams(("parallel",)), name="outproj_ln")(*ins)


def _route_kernel(lg_ref, pos_ref, gate_ref, *, cap, seq):
    lg = lg_ref[0]
    mx = jnp.max(lg, 0, keepdims=True)
    ex = jnp.exp(lg - mx)
    aff = ex / jnp.sum(ex, 0, keepdims=True)
    gate_ref[0] = aff

    def bis(_, carry):
        lo, hi = carry
        mid = lo + ((hi - lo + 1) >> 1)
        cnt = jnp.sum((aff >= pltpu.bitcast(mid, F32)).astype(I32), 1, keepdims=True)
        ok = cnt >= cap
        return jnp.where(ok, mid, lo), jnp.where(ok, hi, mid - 1)

    lo0 = jnp.zeros((N_EXPERTS, 1), I32)
    hi0 = jnp.full((N_EXPERTS, 1), 0x7F800000, I32)
    thr, _ = lax.fori_loop(0, 32, bis, (lo0, hi0))
    gt = aff >= pltpu.bitcast(thr + 1, F32)
    eq = (aff >= pltpu.bitcast(thr, F32)) & jnp.logical_not(gt)
    need = cap - jnp.sum(gt.astype(I32), 1, keepdims=True)

    ck = TOK_CHUNK
    nck = seq // ck
    tri = (lax.broadcasted_iota(I32, (ck, ck), 0) <= lax.broadcasted_iota(I32, (ck, ck), 1)).astype(BF16)

    def chunk_cumsum(mask, k, run):
        mk = mask[:, k * ck:(k + 1) * ck]
        inc = _dot(mk.astype(BF16), tri).astype(I32)
        return mk, inc + run, run + inc[:, ck - 1:ck]

    run = jnp.zeros((N_EXPERTS, 1), I32)
    sel_chunks = []
    for k in range(nck):
        mk, inc, run = chunk_cumsum(eq, k, run)
        sel_chunks.append(gt[:, k * ck:(k + 1) * ck] | (mk & (inc <= need)))

    run = jnp.zeros((N_EXPERTS, 1), I32)
    for k in range(nck):
        mk = sel_chunks[k]
        inc = _dot(mk.astype(BF16), tri).astype(I32) + run
        pos_ref[0, :, k * ck:(k + 1) * ck] = jnp.where(mk, inc - 1, -1)
        run = inc[:, ck - 1:ck]


def _route(logits_t, cap):
    bsz, _, seq = logits_t.shape
    blk = pl.BlockSpec((1, N_EXPERTS, seq), lambda b: (b, 0, 0))
    return pl.pallas_call(
        functools.partial(_route_kernel, cap=cap, seq=seq), grid=(bsz,),
        in_specs=[blk], out_specs=[blk, blk],
        out_shape=[jax.ShapeDtypeStruct((bsz, N_EXPERTS, seq), I32),
                   jax.ShapeDtypeStruct((bsz, N_EXPERTS, seq), F32)],
        compiler_params=_cparams(("parallel",)), name="route_topk")(logits_t)


def _ffn_kernel(pos_ref, gate_ref, h_ref, wg_ref, wu_ref, wd_ref, y_ref, x_ref, gs_ref, acc_ref, *, cap, nck):
    fc = pl.program_id(2)

    @pl.when(fc == 0)
    def _():
        acc_ref[...] = jnp.zeros_like(acc_ref)
        gs_ref[...] = jnp.zeros_like(gs_ref)
        slot = lax.broadcasted_iota(I32, (cap, GATHER_CHUNK), 0)

        def gather(k, _):
            match = slot == pos_ref[0, 0, pl.ds(k, 1), :]
            rows = pl.ds(pl.multiple_of(k * GATHER_CHUNK, GATHER_CHUNK), GATHER_CHUNK)
            acc_ref[...] += _dot(match.astype(BF16), h_ref[0, rows, :])
            gs_ref[...] += jnp.sum(jnp.where(match, gate_ref[0, 0, pl.ds(k, 1), :], 0.0), 1, keepdims=True)
            return 0

        lax.fori_loop(0, nck, gather, 0)
        x_ref[...] = acc_ref[...].astype(BF16)
        acc_ref[...] = jnp.zeros_like(acc_ref)

    x = x_ref[...]
    g = _dot(x, wg_ref[0].astype(BF16))
    u = _dot(x, wu_ref[0].astype(BF16))
    hid = (g * jax.nn.sigmoid(g) * u).astype(BF16)
    acc_ref[...] += _dot(hid, wd_ref[0].astype(BF16))

    @pl.when(fc == pl.num_programs(2) - 1)
    def _():
        y_ref[0, 0] = (acc_ref[...] * gs_ref[...]).astype(BF16)


def _ffn(hb3, pos, gate, w_gate, w_up, w_down, cap):
    bsz, seq, d = hb3.shape
    ne, f = w_gate.shape[0], w_gate.shape[-1]
    nck = seq // GATHER_CHUNK
    fck = FFN_CHUNK
    pos4 = pos.reshape(bsz, ne, nck, GATHER_CHUNK)
    gate4 = gate.reshape(bsz, ne, nck, GATHER_CHUNK)
    tok = pl.BlockSpec((1, 1, nck, GATHER_CHUNK), lambda b, e, c: (b, e, 0, 0))
    return pl.pallas_call(
        functools.partial(_ffn_kernel, cap=cap, nck=nck), grid=(bsz, ne, f // fck),
        in_specs=[tok, tok,
                  pl.BlockSpec((1, seq, d), lambda b, e, c: (b, 0, 0)),
                  pl.BlockSpec((1, d, fck), lambda b, e, c: (e, 0, c)),
                  pl.BlockSpec((1, d, fck), lambda b, e, c: (e, 0, c)),
                  pl.BlockSpec((1, fck, d), lambda b, e, c: (e, c, 0))],
        out_specs=pl.BlockSpec((1, 1, cap, d), lambda b, e, c: (b, e, 0, 0)),
        out_shape=jax.ShapeDtypeStruct((bsz, ne, cap, d), BF16),
        scratch_shapes=[pltpu.VMEM((cap, d), BF16), pltpu.VMEM((cap, 1), F32), pltpu.VMEM((cap, d), F32)],
        compiler_params=_cparams(("parallel", "arbitrary", "arbitrary")), name="moe_ffn")(
            pos4, gate4, hb3, w_gate, w_up, w_down)


def _combine_kernel(pos_ref, y_ref, h_ref, g_ref, b_ref, hn_ref, hb_ref, *, cap, group):
    tt = h_ref.shape[1]
    ne = y_ref.shape[1]
    d = y_ref.shape[-1]
    slot = lax.broadcasted_iota(I32, (group, cap, tt), 1)
    acc = jnp.zeros((tt, d), F32)
    for e0 in range(0, ne, group):
        onehot = (slot == pos_ref[0, e0:e0 + group]).astype(BF16).reshape(group * cap, tt)
        ys = y_ref[0, e0:e0 + group].reshape(group * cap, d)
        acc = acc + _dot_tn(onehot, ys)
    hn = _layer_norm(DN_ALPHA * h_ref[0] + acc, g_ref[...], b_ref[...])
    hn_ref[0] = hn
    hb_ref[0] = hn.astype(BF16)


def _combine(y, pos, h3, g, b):
    bsz, ne, cap, d = y.shape
    seq = h3.shape[1]
    tt = COMBINE_TOKENS
    pos4 = pos.reshape(bsz, ne, 1, seq)
    g2, b2 = g.reshape(1, -1).astype(F32), b.reshape(1, -1).astype(F32)
    hblk = pl.BlockSpec((1, tt, d), lambda bb, k: (bb, k, 0))
    vec = pl.BlockSpec((1, d), lambda bb, k: (0, 0))
    return pl.pallas_call(
        functools.partial(_combine_kernel, cap=cap, group=4), grid=(bsz, seq // tt),
        in_specs=[pl.BlockSpec((1, ne, 1, tt), lambda bb, k: (bb, 0, 0, k)),
                  pl.BlockSpec((1, ne, cap, d), lambda bb, k: (bb, 0, 0, 0), pipeline_mode=pl.Buffered(1)),
                  hblk, vec, vec],
        out_specs=[hblk, hblk],
        out_shape=[jax.ShapeDtypeStruct(h3.shape, F32), jax.ShapeDtypeStruct(h3.shape, BF16)],
        compiler_params=_cparams(("parallel", "arbitrary")), name="moe_combine_ln")(pos4, y, h3, g2, b2)


def _moe_block(h2, hb2, logits_t, w_gate, w_up, w_down, g, b, bsz, seq):
    cap = 2 * seq // N_EXPERTS
    pos, gate = _route(logits_t, cap)
    y = _ffn(hb2.reshape(bsz, seq, D_MODEL), pos, gate, w_gate, w_up, w_down, cap)
    hn, hnb = _combine(y, pos, h2.reshape(bsz, seq, D_MODEL), g, b)
    return hn.reshape(bsz * seq, D_MODEL), hnb.reshape(bsz * seq, D_MODEL)


def _qkv_kernel(x_ref, wqt_ref, wkv_ref, qt_ref, k_ref, v_ref):
    x = x_ref[...]
    w = C_HEADS * HEAD_DIM
    qt_ref[0] = (_dot_nt(wqt_ref[...], x) * (HEAD_DIM ** -0.5 * LOG2E)).astype(BF16)
    k_ref[...] = _dot(x, wkv_ref[:, :w]).astype(BF16)
    v_ref[...] = _dot(x, wkv_ref[:, w:]).astype(BF16)


def _qkv(hb2, w_qkv, bsz, seq):
    m = hb2.shape[0]
    tm = ROW_TILE
    nseq = seq // tm
    w = C_HEADS * HEAD_DIM
    wqt = w_qkv[:, :w].T.astype(BF16)
    wkv = w_qkv[:, w:].astype(BF16)
    row = pl.BlockSpec((tm, w), lambda i: (i, 0))
    return pl.pallas_call(
        _qkv_kernel, grid=(m // tm,),
        in_specs=[pl.BlockSpec((tm, D_MODEL), lambda i: (i, 0)),
                  pl.BlockSpec((w, D_MODEL), lambda i: (0, 0)),
                  pl.BlockSpec((D_MODEL, 2 * w), lambda i: (0, 0))],
        out_specs=[pl.BlockSpec((1, w, tm), lambda i: (i // nseq, 0, i % nseq)), row, row],
        out_shape=[jax.ShapeDtypeStruct((bsz, w, seq), BF16),
                   jax.ShapeDtypeStruct((m, w), BF16), jax.ShapeDtypeStruct((m, w), BF16)],
        compiler_params=_cparams(("parallel",)), name="qkv_proj")(hb2, wqt, wkv)


def _na_window_mask(r0, rows):
    nq, nk = C_QROWS * GRID_W, C_KROWS * GRID_W
    kr0 = jnp.clip(r0 - C_WIN_H // 2, 0, rows - C_KROWS)
    ki = lax.broadcasted_iota(I32, (nk, nq), 0)
    qi = lax.broadcasted_iota(I32, (nk, nq), 1)
    qrow, qcol = r0 + qi // GRID_W, qi % GRID_W
    krow, kcol = kr0 + ki // GRID_W, ki % GRID_W
    rs = jnp.clip(qrow - C_WIN_H // 2, 0, rows - C_WIN_H)
    cs = jnp.clip(qcol - C_WIN_W // 2, 0, GRID_W - C_WIN_W)
    valid = (krow >= rs) & (krow < rs + C_WIN_H) & (kcol >= cs) & (kcol < cs + C_WIN_W)
    return jnp.where(valid, 0.0, NEG_INF).astype(F32)


def _na_kernel(qt_ref, k_ref, v_ref, t2_ref, mask_ref, o_ref, s_ref, *, rows):
    rb = pl.program_id(2)
    nq = C_QROWS * GRID_W
    nk = C_KROWS * GRID_W
    r0 = rb * C_QROWS
    kr0 = jnp.clip(r0 - C_WIN_H // 2, 0, rows - C_KROWS)
    kstart = pl.multiple_of(kr0 * GRID_W, GRID_W)
    mask_add = mask_ref[0]
    dbase = kr0 - r0 + C_WIN_H - 1
    feat = lax.broadcasted_iota(I32, (PAIR, nq), 0)
    for jp in range(C_PAIRS):
        psl = slice(jp * PAIR, (jp + 1) * PAIR)
        qt_pair = qt_ref[0, psl, :]
        k_pair = k_ref[0, pl.ds(kstart, nk), psl]
        for u in range(2):
            qm = jnp.where((feat >= HEAD_DIM) == (u == 1), qt_pair, jnp.zeros_like(qt_pair))
            s_ref[2 * jp + u] = _dot(k_pair, qm)
    for jp in range(C_PAIRS):
        psl = slice(jp * PAIR, (jp + 1) * PAIR)
        v_pair = v_ref[0, pl.ds(kstart, nk), psl]
        outs = []
        for u in range(2):
            h = 2 * jp + u
            bias_rows = []
            for i in range(C_KROWS):
                tiles = [t2_ref[h, jnp.clip(dbase + i - 2 * jq, 0, 2 * C_WIN_H - 1)] for jq in range(C_QROWS // 2)]
                bias_rows.append(jnp.concatenate(tiles, 1))
            s = s_ref[h] + jnp.concatenate(bias_rows, 0) + mask_add
            m = jnp.max(s, 0, keepdims=True)
            p = jnp.exp2(s - m)
            l = jnp.sum(p, 0, keepdims=True)
            outs.append(_dot_tn(v_pair, p.astype(BF16)) / l)
        o_ref[0, psl, :] = jnp.where(feat < HEAD_DIM, outs[0], outs[1]).astype(BF16)


def _natten(qt3, k3, v3, rpb):
    bsz, seq, _ = k3.shape
    rows = seq // GRID_W
    hps = 2 * C_PAIRS
    cols = jnp.arange(GRID_W)
    dc = jnp.clip(cols[:, None] - cols[None, :] + C_WIN_W - 1, 0, 2 * C_WIN_W - 2)
    tt = (rpb.astype(F32) * LOG2E)[:, :, dc]
    zero = jnp.zeros_like(tt[:, :1])
    text = jnp.concatenate([zero, tt, zero], 1)
    t2 = jnp.concatenate([text[:, 1:], text[:, :-1]], -1)
    nd = t2.shape[1]
    nq = C_QROWS * GRID_W
    nk = C_KROWS * GRID_W
    nrb = rows // C_QROWS
    assert nrb >= 3
    masks = jnp.stack([_na_window_mask(r0, rows) for r0 in (0, C_QROWS, rows - C_QROWS)])
    w = C_PAIRS * PAIR
    qblk = pl.BlockSpec((1, w, nq), lambda b, g, r: (b, g, r))
    kblk = pl.BlockSpec((1, seq, w), lambda b, g, r: (b, 0, g))
    return pl.pallas_call(
        functools.partial(_na_kernel, rows=rows),
        grid=(bsz, C_HEADS // hps, nrb),
        in_specs=[qblk, kblk, kblk,
                  pl.BlockSpec((hps, nd, GRID_W, 2 * GRID_W), lambda b, g, r: (g, 0, 0, 0)),
                  pl.BlockSpec((1, nk, nq), lambda b, g, r: ((r > 0).astype(I32) + (r == nrb - 1).astype(I32), 0, 0))],
        out_specs=qblk, out_shape=jax.ShapeDtypeStruct((bsz, C_HEADS * HEAD_DIM, seq), BF16),
        scratch_shapes=[pltpu.VMEM((hps, C_KROWS * GRID_W, nq), F32)],
        compiler_params=_cparams(("parallel", "parallel", "arbitrary")), name="natten")(qt3, k3, v3, t2, masks)


def kernel(x, w_in0, a_sink, mla_q_norm, w_q_up, mla_kv_norm, w_kv_up, w_out0, ln0a_g, ln0a_b,
           router0, w_gate0, w_up0, w_down0, ln0b_g, ln0b_b,
           w_qkv1, na_rpb, w_out1, ln1a_g, ln1a_b,
           router1, w_gate1, w_up1, w_down1, ln1b_g, ln1b_b):
    bsz, seq, d = x.shape
    m = bsz * seq
    x2 = x.reshape(m, d)
    qa, ka, va, qt, km, vt = _even_proj(x2, w_in0, mla_q_norm, w_q_up, mla_kv_norm, w_kv_up, bsz, seq)
    out_a = _swa(qa.reshape(bsz, seq, -1), ka.reshape(bsz, seq, -1), va.reshape(bsz, seq, -1), a_sink)
    out_bt = _mla(qt, km.reshape(bsz, seq, -1), vt)
    h, hb, lg = _outproj([out_a.reshape(m, A_Q)], [w_out0[:A_Q]], [out_bt], [w_out0[A_Q:]],
                         x2, ln0a_g, ln0a_b, router0, bsz, seq)
    h, hb = _moe_block(h, hb, lg, w_gate0, w_up0, w_down0, ln0b_g, ln0b_b, bsz, seq)
    qt1, k2, v2 = _qkv(hb, w_qkv1, bsz, seq)
    na_t = _natten(qt1, k2.reshape(bsz, seq, -1), v2.reshape(bsz, seq, -1), na_rpb)
    h, hb, lg = _outproj([], [], [na_t.reshape(bsz, C_HEADS, HEAD_DIM, seq)], [w_out1],
                         h, ln1a_g, ln1a_b, router1, bsz, seq)
    h, _ = _moe_block(h, hb, lg, w_gate1, w_up1, w_down1, ln1b_g, ln1b_b, bsz, seq)
    return h.reshape(bsz, seq, d)
```

```python
import functools
import math

import jax
import jax.numpy as jnp
from jax import lax
from jax.experimental import pallas as pl
from jax.experimental.pallas import tpu as pltpu

F32 = jnp.float32
BF16 = jnp.bfloat16
I32 = jnp.int32

D_MODEL = 1024
DEPTH = 2
HEAD_DIM = 64
NEG_INF = -1e30
A_HEADS = 8
A_KV_HEADS = 2
A_REP = A_HEADS // A_KV_HEADS
A_WINDOW = 128
A_Q = A_HEADS * HEAD_DIM
A_KV = A_KV_HEADS * HEAD_DIM
A_TQ = 256
B_HEADS = 8
B_Q_RANK = 384
B_KV_RANK = 256
B_NOPE = 64
B_ROPE = 32
B_V = 64
B_PAD = 128
B_TQ = 512
ROPE_BASE = 10000.0
C_HEADS = 16
GRID_W = 64
C_WIN_H = 8
C_WIN_W = 16
C_QROWS = 4
C_KROWS = 12
C_PAIRS = 4
N_EXPERTS = 16
D_EXPERT = 2 * D_MODEL
TOK_CHUNK = 256
GATHER_CHUNK = 1024
COMBINE_TOKENS = 1024
FFN_CHUNK = 1024
LN_EPS = 1e-5
RMS_EPS = 1e-6
DN_ALPHA = (2 * DEPTH) ** 0.25

LOG2E = math.log2(math.e)
LANES = 128
PAIR = 2 * HEAD_DIM
VMEM_LIMIT = 56 * 1024 * 1024
ROW_TILE = 512


def _cparams(sem):
    return pltpu.CompilerParams(dimension_semantics=sem, vmem_limit_bytes=VMEM_LIMIT)


def _layer_norm(z, g, b):
    mu = jnp.mean(z, -1, keepdims=True)
    zc = z - mu
    var = jnp.mean(zc * zc, -1, keepdims=True)
    return zc * lax.rsqrt(var + LN_EPS) * g + b


def _rms_norm(c, g):
    return c * lax.rsqrt(jnp.mean(c * c, -1, keepdims=True) + RMS_EPS) * g


def _bf16_part(x):
    return pltpu.bitcast(pltpu.bitcast(x, I32) & jnp.int32(-65536), F32)


def _dot(a, b):
    return jnp.dot(a, b, preferred_element_type=F32)


def _dot_nt(a, b):
    return lax.dot_general(a, b, (((1,), (1,)), ((), ())), preferred_element_type=F32)


def _dot_tn(a, b):
    return lax.dot_general(a, b, (((0,), (0,)), ((), ())), preferred_element_type=F32)


def _own_half(lo, hi):
    lane = lax.broadcasted_iota(I32, lo.shape, 1)
    return jnp.where(lane < HEAD_DIM, lo, hi)


def _even_proj_kernel(x_ref, wa_ref, wcq_ref, wckv_ref, wkpr_ref, gq_ref, gkv_ref,
                      wq1t_ref, wkn_ref, wvt_ref, cos_ref, sin_ref, cost_ref, sint_ref,
                      qa_ref, ka_ref, va_ref, qt_ref, k_ref, vt_ref, *, q_scale):
    xb = x_ref[...].astype(BF16)
    a = _dot(xb, wa_ref[...])
    qa_ref[...] = (a[:, :A_Q] * (HEAD_DIM ** -0.5 * LOG2E)).astype(BF16)
    ka_ref[...] = a[:, A_Q:A_Q + 2 * A_KV].astype(BF16)
    va_ref[...] = a[:, A_Q + 2 * A_KV:].astype(BF16)

    nq = _rms_norm(_dot(xb, wcq_ref[...]), gq_ref[...]).astype(BF16)
    qpt = _dot_nt(wq1t_ref[...], nq)
    cost, sint = cost_ref[...], sint_ref[...]
    half = B_ROPE // 2
    tm = qpt.shape[-1]
    for h in range(B_HEADS):
        blk = qpt[h * B_PAD:(h + 1) * B_PAD]
        rot = jnp.concatenate([jnp.zeros((B_NOPE, tm), F32), -blk[B_NOPE + half:B_NOPE + B_ROPE],
                               blk[B_NOPE:B_NOPE + half], jnp.zeros((B_PAD - B_NOPE - B_ROPE, tm), F32)], 0)
        qt_ref[0, h] = ((blk * cost + rot * sint) * q_scale).astype(BF16)

    nkv = _rms_norm(_dot(xb, wckv_ref[...]), gkv_ref[...]).astype(BF16)
    kn = _dot(nkv, wkn_ref[...])
    kpr = _dot(xb, wkpr_ref[...])
    kpe = kpr[:, :B_PAD] * cos_ref[...] + kpr[:, B_PAD:] * sin_ref[...]
    for h in range(B_HEADS):
        sl = slice(h * B_PAD, (h + 1) * B_PAD)
        k_ref[:, sl] = (kn[:, sl] + kpe).astype(BF16)
    vt = _dot_nt(wvt_ref[...], nkv)
    vt_ref[0, :, 0] = vt.reshape(B_HEADS, B_V, vt.shape[-1]).astype(BF16)


def _even_proj(x2, w_in, gq, w_q_up, gkv, w_kv_up, bsz, seq):
    m = x2.shape[0]
    tm = ROW_TILE
    nseq = seq // tm
    o1, o2, o3 = A_Q, A_Q + A_KV, A_Q + 2 * A_KV
    o4, o5 = o3 + B_Q_RANK, o3 + B_Q_RANK + B_KV_RANK
    half = B_ROPE // 2
    wk = w_in[:, o1:o2].reshape(D_MODEL, A_KV_HEADS, 1, HEAD_DIM)
    wv = w_in[:, o2:o3].reshape(D_MODEL, A_KV_HEADS, 1, HEAD_DIM)
    dup = lambda w: jnp.broadcast_to(w, (D_MODEL, A_KV_HEADS, 2, HEAD_DIM)).reshape(D_MODEL, 2 * A_KV)
    wa = jnp.concatenate([w_in[:, :o1], dup(wk), dup(wv)], 1).astype(BF16)
    wcq = w_in[:, o3:o4].astype(BF16)
    wckv = w_in[:, o4:o5].astype(BF16)
    wkr_raw = w_in[:, o5:]
    zpad = jnp.zeros((D_MODEL, B_PAD - B_NOPE - B_ROPE), F32)
    znope = jnp.zeros((D_MODEL, B_NOPE), F32)
    wkpr = jnp.concatenate([znope, wkr_raw, zpad,
                            znope, -wkr_raw[:, half:], wkr_raw[:, :half], zpad], 1).astype(BF16)
    wq = w_q_up.reshape(B_Q_RANK, B_HEADS, B_NOPE + B_ROPE)
    zq = jnp.zeros((B_Q_RANK, B_HEADS, B_PAD - B_NOPE - B_ROPE), F32)
    wq1t = jnp.concatenate([wq, zq], -1).reshape(B_Q_RANK, B_HEADS * B_PAD).T.astype(BF16)
    wkv = w_kv_up.reshape(B_KV_RANK, B_HEADS, B_NOPE + B_V)
    zk = jnp.zeros((B_KV_RANK, B_HEADS, B_PAD - B_NOPE), F32)
    wkn = jnp.concatenate([wkv[..., :B_NOPE], zk], -1).reshape(B_KV_RANK, B_HEADS * B_PAD).astype(BF16)
    wvt = wkv[..., B_NOPE:].reshape(B_KV_RANK, B_HEADS * B_V).T.astype(BF16)
    inv = ROPE_BASE ** (-jnp.arange(half, dtype=F32) / half)
    inv_l = jnp.concatenate([jnp.zeros((B_NOPE,), F32), inv, inv, jnp.zeros((B_PAD - B_NOPE - B_ROPE,), F32)])
    ang = jnp.arange(seq, dtype=I32).astype(F32)[:, None] * inv_l[None, :]
    cos_t, sin_t = jnp.cos(ang), jnp.sin(ang)

    full = lambda a: pl.BlockSpec(a.shape, lambda i: (0,) * a.ndim)
    row = lambda w: pl.BlockSpec((tm, w), lambda i: (i, 0))
    tab = pl.BlockSpec((tm, B_PAD), lambda i: (i % nseq, 0))
    tab_t = pl.BlockSpec((B_PAD, tm), lambda i: (0, i % nseq))
    gq2, gkv2 = gq.reshape(1, -1).astype(F32), gkv.reshape(1, -1).astype(F32)
    ins = [x2, wa, wcq, wckv, wkpr, gq2, gkv2, wq1t, wkn, wvt, cos_t, sin_t, cos_t.T, sin_t.T]
    in_specs = [row(D_MODEL)] + [full(a) for a in ins[1:10]] + [tab, tab, tab_t, tab_t]
    out_shape = [jax.ShapeDtypeStruct((m, A_Q), BF16),
                 jax.ShapeDtypeStruct((m, 2 * A_KV), BF16), jax.ShapeDtypeStruct((m, 2 * A_KV), BF16),
                 jax.ShapeDtypeStruct((bsz, B_HEADS, B_PAD, seq), BF16),
                 jax.ShapeDtypeStruct((m, B_HEADS * B_PAD), BF16),
                 jax.ShapeDtypeStruct((bsz, B_HEADS, nseq, B_V, tm), BF16)]
    out_specs = [row(A_Q), row(2 * A_KV), row(2 * A_KV),
                 pl.BlockSpec((1, B_HEADS, B_PAD, tm), lambda i: (i // nseq, 0, 0, i % nseq)),
                 row(B_HEADS * B_PAD),
                 pl.BlockSpec((1, B_HEADS, 1, B_V, tm), lambda i: (i // nseq, 0, i % nseq, 0, 0))]
    q_scale = (B_NOPE + B_ROPE) ** -0.5 * LOG2E
    return pl.pallas_call(
        functools.partial(_even_proj_kernel, q_scale=q_scale),
        grid=(m // tm,), in_specs=in_specs, out_specs=out_specs, out_shape=out_shape,
        compiler_params=_cparams(("parallel",)), name="even_proj")(*ins)


def _swa_bias(start, seq):
    tq = A_TQ
    span = tq + 2 * A_WINDOW
    t = start + lax.broadcasted_iota(I32, (tq, span), 0)
    src = start - A_WINDOW + lax.broadcasted_iota(I32, (tq, span), 1)
    dist = jnp.abs(t - src)
    valid = (dist <= A_WINDOW) & (src >= 0) & (src < seq)
    slopes = 2.0 ** (-8.0 * (jnp.arange(A_HEADS, dtype=F32) + 1.0) / A_HEADS) * LOG2E
    return jnp.where(valid[None], -slopes[:, None, None] * dist.astype(F32)[None], NEG_INF)


def _swa_kernel(sink_ref, q_ref, kp_ref, kc_ref, kn_ref, vp_ref, vc_ref, vn_ref, bias_ref, o_ref, s_ref):
    tq = A_TQ
    kb = jnp.concatenate([kp_ref[0], kc_ref[0], kn_ref[0]], 0)
    vb = jnp.concatenate([vp_ref[0], vc_ref[0], vn_ref[0]], 0)
    lane = lax.broadcasted_iota(I32, (tq, PAIR), 1)
    outs = []
    for h in range(A_HEADS):
        g = h // A_REP
        q_pair = q_ref[0, :, (h // 2) * PAIR:(h // 2 + 1) * PAIR]
        qm = jnp.where((lane >= HEAD_DIM) == (h % 2 == 1), q_pair, jnp.zeros_like(q_pair))
        s_ref[h] = _dot_nt(qm, kb[:, g * PAIR:(g + 1) * PAIR])
    for h in range(A_HEADS):
        g = h // A_REP
        s = s_ref[h] + bias_ref[0, h]
        sink = sink_ref[h]
        m = jnp.maximum(jnp.max(s, -1, keepdims=True), sink)
        p = jnp.exp2(s - m)
        denom = jnp.sum(p, -1, keepdims=True) + jnp.exp2(sink - m)
        outs.append(_dot(p.astype(BF16), vb[:, g * PAIR:(g + 1) * PAIR]) / denom)
    for jp in range(A_HEADS // 2):
        o_ref[0, :, jp * PAIR:(jp + 1) * PAIR] = _own_half(outs[2 * jp], outs[2 * jp + 1]).astype(BF16)


def _swa(qa3, ka3, va3, a_sink):
    bsz, seq, _ = qa3.shape
    tq, w = A_TQ, A_WINDOW
    nblk = seq // w
    r = tq // w
    kw = 2 * A_KV
    prev = pl.BlockSpec((1, w, kw), lambda b, i: (b, jnp.maximum(i * r - 1, 0), 0))
    cur = pl.BlockSpec((1, tq, kw), lambda b, i: (b, i, 0))
    nxt = pl.BlockSpec((1, w, kw), lambda b, i: (b, jnp.minimum(i * r + r, nblk - 1), 0))
    qblk = pl.BlockSpec((1, tq, A_Q), lambda b, i: (b, i, 0))
    nstep = seq // tq
    assert nstep >= 3
    bias = jnp.stack([_swa_bias(s0, seq) for s0 in (0, tq, seq - tq)])
    bblk = pl.BlockSpec((1, A_HEADS, tq, tq + 2 * w),
                        lambda b, i: ((i > 0).astype(I32) + (i == nstep - 1).astype(I32), 0, 0, 0))
    return pl.pallas_call(
        _swa_kernel, grid=(bsz, nstep),
        in_specs=[pl.BlockSpec(memory_space=pltpu.SMEM), qblk, prev, cur, nxt, prev, cur, nxt, bblk],
        out_specs=qblk, out_shape=jax.ShapeDtypeStruct((bsz, seq, A_Q), BF16),
        scratch_shapes=[pltpu.VMEM((A_HEADS, tq, tq + 2 * w), F32)],
        compiler_params=_cparams(("parallel", "arbitrary")), name="swa_attn")(
            a_sink.astype(F32) * LOG2E, qa3, ka3, ka3, ka3, va3, va3, va3, bias)


def _mla_kernel(qt_ref, k_ref, vt_ref, o_ref, sa_ref, sb_ref, cma_ref, cmb_ref, m_ref, l_ref, acc_ref,
                *, tk, n_chunks):
    m_ref[...] = jnp.full_like(m_ref, NEG_INF)
    l_ref[...] = jnp.zeros_like(l_ref)
    acc_ref[...] = jnp.zeros_like(acc_ref)

    def scores(c, s_ref, cm_ref):
        rows = pl.ds(pl.multiple_of(c * tk, tk), tk)
        for h in range(B_HEADS):
            s = _dot(k_ref[0, rows, h * B_PAD:(h + 1) * B_PAD], qt_ref[0, h])
            s_ref[h] = s
            cm_ref[h] = jnp.max(s, 0, keepdims=True)

    def update(c, s_ref, cm_ref):
        for h in range(B_HEADS):
            m_old = m_ref[h]
            m_new = jnp.maximum(m_old, cm_ref[h])
            alpha = jnp.exp2(m_old - m_new)
            p = jnp.exp2(s_ref[h] - m_new)
            l_ref[h] = l_ref[h] * alpha + jnp.sum(p, 0, keepdims=True)
            acc_ref[h] = acc_ref[h] * alpha + _dot(vt_ref[0, h, c], p.astype(BF16))
            m_ref[h] = m_new

    a, b = (sa_ref, cma_ref), (sb_ref, cmb_ref)
    scores(0, *a)

    def body(i, _):
        scores(2 * i + 1, *b)
        update(2 * i, *a)
        scores(2 * i + 2, *a)
        update(2 * i + 1, *b)
        return 0

    lax.fori_loop(0, n_chunks // 2 - 1, body, 0)
    scores(n_chunks - 1, *b)
    update(n_chunks - 2, *a)
    update(n_chunks - 1, *b)
    for h in range(B_HEADS):
        o_ref[0, h] = (acc_ref[h] / l_ref[h]).astype(BF16)


def _mla(qt, k3, vt):
    bsz, _, _, seq = qt.shape
    tq = B_TQ
    nck, tk = vt.shape[2], vt.shape[4]
    return pl.pallas_call(
        functools.partial(_mla_kernel, tk=tk, n_chunks=nck),
        grid=(bsz, seq // tq),
        in_specs=[pl.BlockSpec((1, B_HEADS, B_PAD, tq), lambda b, i: (b, 0, 0, i)),
                  pl.BlockSpec((1, seq, B_HEADS * B_PAD), lambda b, i: (b, 0, 0)),
                  pl.BlockSpec((1, B_HEADS, nck, B_V, tk), lambda b, i: (b, 0, 0, 0, 0))],
        out_specs=pl.BlockSpec((1, B_HEADS, B_V, tq), lambda b, i: (b, 0, 0, i)),
        out_shape=jax.ShapeDtypeStruct((bsz, B_HEADS, B_V, seq), BF16),
        scratch_shapes=[pltpu.VMEM((B_HEADS, tk, tq), F32), pltpu.VMEM((B_HEADS, tk, tq), F32),
                        pltpu.VMEM((B_HEADS, 1, tq), F32), pltpu.VMEM((B_HEADS, 1, tq), F32),
                        pltpu.VMEM((B_HEADS, 1, tq), F32), pltpu.VMEM((B_HEADS, 1, tq), F32),
                        pltpu.VMEM((B_HEADS, B_V, tq), F32)],
        compiler_params=_cparams(("parallel", "arbitrary")), name="mla_attn")(qt, k3, vt)


def _outproj_kernel(*refs, n_row, n_t):
    n_in = n_row + n_t
    a_refs = refs[:n_in]
    w_refs = refs[n_in:2 * n_in]
    h_ref, g_ref, b_ref, wr_ref = refs[2 * n_in:2 * n_in + 4]
    hn_ref, hb_ref, lg_ref = refs[2 * n_in + 4:]
    mix = None
    for a_ref, w_ref in zip(a_refs[:n_row], w_refs[:n_row]):
        part = _dot(a_ref[...], w_ref[...])
        mix = part if mix is None else mix + part
    for a_ref, w_ref in zip(a_refs[n_row:], w_refs[n_row:]):
        at = a_ref[0]
        part = _dot_tn(at.reshape(at.shape[0] * at.shape[1], at.shape[2]), w_ref[...])
        mix = part if mix is None else mix + part
    hn = _layer_norm(DN_ALPHA * h_ref[...] + mix, g_ref[...], b_ref[...])
    hn_ref[...] = hn
    hb_ref[...] = hn.astype(BF16)
    hi_f = _bf16_part(hn)
    whl = wr_ref[...]
    parts = _dot_nt(whl, hi_f.astype(BF16)) + _dot_nt(whl, (hn - hi_f).astype(BF16))
    lg_ref[0] = parts[:N_EXPERTS] + parts[N_EXPERTS:]


def _outproj(rows, rows_w, trans, trans_w, h2, g, b, w_router, bsz, seq):
    m = h2.shape[0]
    tm = ROW_TILE
    nseq = seq // tm
    w_list = [w.astype(BF16) for w in list(rows_w) + list(trans_w)]
    wr_f = w_router.T.astype(F32)
    wr_hi = lax.bitcast_convert_type(lax.bitcast_convert_type(wr_f, jnp.uint32) & jnp.uint32(0xFFFF0000), F32)
    wr_t = jnp.concatenate([wr_hi, wr_f - wr_hi], 0).astype(BF16)
    row = lambda w: pl.BlockSpec((tm, w), lambda i: (i, 0))
    full = lambda a: pl.BlockSpec(a.shape, lambda i: (0,) * a.ndim)
    tblk = lambda a: pl.BlockSpec((1, a.shape[1], a.shape[2], tm), lambda i: (i // nseq, 0, 0, i % nseq))
    g2, b2 = g.reshape(1, -1).astype(F32), b.reshape(1, -1).astype(F32)
    ins = list(rows) + list(trans) + w_list + [h2, g2, b2, wr_t]
    in_specs = ([row(a.shape[1]) for a in rows] + [tblk(a) for a in trans] + [full(w) for w in w_list]
                + [row(D_MODEL), full(g2), full(b2), full(wr_t)])
    out_shape = [jax.ShapeDtypeStruct((m, D_MODEL), F32), jax.ShapeDtypeStruct((m, D_MODEL), BF16),
                 jax.ShapeDtypeStruct((bsz, N_EXPERTS, seq), F32)]
    out_specs = [row(D_MODEL), row(D_MODEL),
                 pl.BlockSpec((1, N_EXPERTS, tm), lambda i: (i // nseq, 0, i % nseq))]
    return pl.pallas_call(
        functools.partial(_outproj_kernel, n_row=len(rows), n_t=len(trans)), grid=(m // tm,),
        in_specs=in_specs, out_specs=out_specs, out_shape=out_shape,
        compiler_params=_cparams(("parallel",)), name="outproj_ln")(*ins)


def _route_kernel(lg_ref, pos_ref, gate_ref, *, cap, seq):
    lg = lg_ref[0]
    mx = jnp.max(lg, 0, keepdims=True)
    ex = jnp.exp(lg - mx)
    aff = ex / jnp.sum(ex, 0, keepdims=True)
    gate_ref[0] = aff

    def bis(_, carry):
        lo, hi = carry
        mid = lo + ((hi - lo + 1) >> 1)
        cnt = jnp.sum((aff >= pltpu.bitcast(mid, F32)).astype(I32), 1, keepdims=True)
        ok = cnt >= cap
        return jnp.where(ok, mid, lo), jnp.where(ok, hi, mid - 1)

    lo0 = jnp.zeros((N_EXPERTS, 1), I32)
    hi0 = jnp.full((N_EXPERTS, 1), 0x7F800000, I32)
    thr, _ = lax.fori_loop(0, 32, bis, (lo0, hi0))
    gt = aff >= pltpu.bitcast(thr + 1, F32)
    eq = (aff >= pltpu.bitcast(thr, F32)) & jnp.logical_not(gt)
    need = cap - jnp.sum(gt.astype(I32), 1, keepdims=True)

    ck = TOK_CHUNK
    nck = seq // ck
    tri = (lax.broadcasted_iota(I32, (ck, ck), 0) <= lax.broadcasted_iota(I32, (ck, ck), 1)).astype(BF16)

    def chunk_cumsum(mask, k, run):
        mk = mask[:, k * ck:(k + 1) * ck]
        inc = _dot(mk.astype(BF16), tri).astype(I32)
        return mk, inc + run, run + inc[:, ck - 1:ck]

    run = jnp.zeros((N_EXPERTS, 1), I32)
    sel_chunks = []
    for k in range(nck):
        mk, inc, run = chunk_cumsum(eq, k, run)
        sel_chunks.append(gt[:, k * ck:(k + 1) * ck] | (mk & (inc <= need)))

    run = jnp.zeros((N_EXPERTS, 1), I32)
    for k in range(nck):
        mk = sel_chunks[k]
        inc = _dot(mk.astype(BF16), tri).astype(I32) + run
        pos_ref[0, :, k * ck:(k + 1) * ck] = jnp.where(mk, inc - 1, -1)
        run = inc[:, ck - 1:ck]


def _route(logits_t, cap):
    bsz, _, seq = logits_t.shape
    blk = pl.BlockSpec((1, N_EXPERTS, seq), lambda b: (b, 0, 0))
    return pl.pallas_call(
        functools.partial(_route_kernel, cap=cap, seq=seq), grid=(bsz,),
        in_specs=[blk], out_specs=[blk, blk],
        out_shape=[jax.ShapeDtypeStruct((bsz, N_EXPERTS, seq), I32),
                   jax.ShapeDtypeStruct((bsz, N_EXPERTS, seq), F32)],
        compiler_params=_cparams(("parallel",)), name="route_topk")(logits_t)


def _ffn_kernel(pos_ref, gate_ref, h_ref, wg_ref, wu_ref, wd_ref, y_ref, x_ref, gs_ref, acc_ref, *, cap, nck):
    fc = pl.program_id(2)

    @pl.when(fc == 0)
    def _():
        acc_ref[...] = jnp.zeros_like(acc_ref)
        gs_ref[...] = jnp.zeros_like(gs_ref)
        slot = lax.broadcasted_iota(I32, (cap, GATHER_CHUNK), 0)

        def gather(k, _):
            match = slot == pos_ref[0, 0, pl.ds(k, 1), :]
            rows = pl.ds(pl.multiple_of(k * GATHER_CHUNK, GATHER_CHUNK), GATHER_CHUNK)
            acc_ref[...] += _dot(match.astype(BF16), h_ref[0, rows, :])
            gs_ref[...] += jnp.sum(jnp.where(match, gate_ref[0, 0, pl.ds(k, 1), :], 0.0), 1, keepdims=True)
            return 0

        lax.fori_loop(0, nck, gather, 0)
        x_ref[...] = acc_ref[...].astype(BF16)
        acc_ref[...] = jnp.zeros_like(acc_ref)

    x = x_ref[...]
    g = _dot(x, wg_ref[0].astype(BF16))
    u = _dot(x, wu_ref[0].astype(BF16))
    hid = (g * jax.nn.sigmoid(g) * u).astype(BF16)
    acc_ref[...] += _dot(hid, wd_ref[0].astype(BF16))

    @pl.when(fc == pl.num_programs(2) - 1)
    def _():
        y_ref[0, 0] = (acc_ref[...] * gs_ref[...]).astype(BF16)


def _ffn(hb3, pos, gate, w_gate, w_up, w_down, cap):
    bsz, seq, d = hb3.shape
    ne, f = w_gate.shape[0], w_gate.shape[-1]
    nck = seq // GATHER_CHUNK
    fck = FFN_CHUNK
    pos4 = pos.reshape(bsz, ne, nck, GATHER_CHUNK)
    gate4 = gate.reshape(bsz, ne, nck, GATHER_CHUNK)
    tok = pl.BlockSpec((1, 1, nck, GATHER_CHUNK), lambda b, e, c: (b, e, 0, 0))
    return pl.pallas_call(
        functools.partial(_ffn_kernel, cap=cap, nck=nck), grid=(bsz, ne, f // fck),
        in_specs=[tok, tok,
                  pl.BlockSpec((1, seq, d), lambda b, e, c: (b, 0, 0)),
                  pl.BlockSpec((1, d, fck), lambda b, e, c: (e, 0, c)),
                  pl.BlockSpec((1, d, fck), lambda b, e, c: (e, 0, c)),
                  pl.BlockSpec((1, fck, d), lambda b, e, c: (e, c, 0))],
        out_specs=pl.BlockSpec((1, 1, cap, d), lambda b, e, c: (b, e, 0, 0)),
        out_shape=jax.ShapeDtypeStruct((bsz, ne, cap, d), BF16),
        scratch_shapes=[pltpu.VMEM((cap, d), BF16), pltpu.VMEM((cap, 1), F32), pltpu.VMEM((cap, d), F32)],
        compiler_params=_cparams(("parallel", "arbitrary", "arbitrary")), name="moe_ffn")(
            pos4, gate4, hb3, w_gate, w_up, w_down)


def _combine_kernel(pos_ref, y_ref, h_ref, g_ref, b_ref, hn_ref, hb_ref, *, cap, group):
    tt = h_ref.shape[1]
    ne = y_ref.shape[1]
    d = y_ref.shape[-1]
    slot = lax.broadcasted_iota(I32, (group, cap, tt), 1)
    acc = jnp.zeros((tt, d), F32)
    for e0 in range(0, ne, group):
        onehot = (slot == pos_ref[0, e0:e0 + group]).astype(BF16).reshape(group * cap, tt)
        ys = y_ref[0, e0:e0 + group].reshape(group * cap, d)
        acc = acc + _dot_tn(onehot, ys)
    hn = _layer_norm(DN_ALPHA * h_ref[0] + acc, g_ref[...], b_ref[...])
    hn_ref[0] = hn
    hb_ref[0] = hn.astype(BF16)


def _combine(y, pos, h3, g, b):
    bsz, ne, cap, d = y.shape
    seq = h3.shape[1]
    tt = COMBINE_TOKENS
    pos4 = pos.reshape(bsz, ne, 1, seq)
    g2, b2 = g.reshape(1, -1).astype(F32), b.reshape(1, -1).astype(F32)
    hblk = pl.BlockSpec((1, tt, d), lambda bb, k: (bb, k, 0))
    vec = pl.BlockSpec((1, d), lambda bb, k: (0, 0))
    return pl.pallas_call(
        functools.partial(_combine_kernel, cap=cap, group=4), grid=(bsz, seq // tt),
        in_specs=[pl.BlockSpec((1, ne, 1, tt), lambda bb, k: (bb, 0, 0, k)),
                  pl.BlockSpec((1, ne, cap, d), lambda bb, k: (bb, 0, 0, 0), pipeline_mode=pl.Buffered(1)),
                  hblk, vec, vec],
        out_specs=[hblk, hblk],
        out_shape=[jax.ShapeDtypeStruct(h3.shape, F32), jax.ShapeDtypeStruct(h3.shape, BF16)],
        compiler_params=_cparams(("parallel", "arbitrary")), name="moe_combine_ln")(pos4, y, h3, g2, b2)


def _moe_block(h2, hb2, logits_t, w_gate, w_up, w_down, g, b, bsz, seq):
    cap = 2 * seq // N_EXPERTS
    pos, gate = _route(logits_t, cap)
    y = _ffn(hb2.reshape(bsz, seq, D_MODEL), pos, gate, w_gate, w_up, w_down, cap)
    hn, hnb = _combine(y, pos, h2.reshape(bsz, seq, D_MODEL), g, b)
    return hn.reshape(bsz * seq, D_MODEL), hnb.reshape(bsz * seq, D_MODEL)


def _qkv_kernel(x_ref, wqt_ref, wkv_ref, qt_ref, k_ref, v_ref):
    x = x_ref[...]
    w = C_HEADS * HEAD_DIM
    qt_ref[0] = (_dot_nt(wqt_ref[...], x) * (HEAD_DIM ** -0.5 * LOG2E)).astype(BF16)
    k_ref[...] = _dot(x, wkv_ref[:, :w]).astype(BF16)
    v_ref[...] = _dot(x, wkv_ref[:, w:]).astype(BF16)


def _qkv(hb2, w_qkv, bsz, seq):
    m = hb2.shape[0]
    tm = ROW_TILE
    nseq = seq // tm
    w = C_HEADS * HEAD_DIM
    wqt = w_qkv[:, :w].T.astype(BF16)
    wkv = w_qkv[:, w:].astype(BF16)
    row = pl.BlockSpec((tm, w), lambda i: (i, 0))
    return pl.pallas_call(
        _qkv_kernel, grid=(m // tm,),
        in_specs=[pl.BlockSpec((tm, D_MODEL), lambda i: (i, 0)),
                  pl.BlockSpec((w, D_MODEL), lambda i: (0, 0)),
                  pl.BlockSpec((D_MODEL, 2 * w), lambda i: (0, 0))],
        out_specs=[pl.BlockSpec((1, w, tm), lambda i: (i // nseq, 0, i % nseq)), row, row],
        out_shape=[jax.ShapeDtypeStruct((bsz, w, seq), BF16),
                   jax.ShapeDtypeStruct((m, w), BF16), jax.ShapeDtypeStruct((m, w), BF16)],
        compiler_params=_cparams(("parallel",)), name="qkv_proj")(hb2, wqt, wkv)


def _na_window_mask(r0, rows):
    nq, nk = C_QROWS * GRID_W, C_KROWS * GRID_W
    kr0 = jnp.clip(r0 - C_WIN_H // 2, 0, rows - C_KROWS)
    ki = lax.broadcasted_iota(I32, (nk, nq), 0)
    qi = lax.broadcasted_iota(I32, (nk, nq), 1)
    qrow, qcol = r0 + qi // GRID_W, qi % GRID_W
    krow, kcol = kr0 + ki // GRID_W, ki % GRID_W
    rs = jnp.clip(qrow - C_WIN_H // 2, 0, rows - C_WIN_H)
    cs = jnp.clip(qcol - C_WIN_W // 2, 0, GRID_W - C_WIN_W)
    valid = (krow >= rs) & (krow < rs + C_WIN_H) & (kcol >= cs) & (kcol < cs + C_WIN_W)
    return jnp.where(valid, 0.0, NEG_INF).astype(F32)


def _na_kernel(qt_ref, k_ref, v_ref, t2_ref, mask_ref, o_ref, s_ref, *, rows):
    rb = pl.program_id(2)
    nq = C_QROWS * GRID_W
    nk = C_KROWS * GRID_W
    r0 = rb * C_QROWS
    kr0 = jnp.clip(r0 - C_WIN_H // 2, 0, rows - C_KROWS)
    kstart = pl.multiple_of(kr0 * GRID_W, GRID_W)
    mask_add = mask_ref[0]
    dbase = kr0 - r0 + C_WIN_H - 1
    feat = lax.broadcasted_iota(I32, (PAIR, nq), 0)
    for jp in range(C_PAIRS):
        psl = slice(jp * PAIR, (jp + 1) * PAIR)
        qt_pair = qt_ref[0, psl, :]
        k_pair = k_ref[0, pl.ds(kstart, nk), psl]
        for u in range(2):
            qm = jnp.where((feat >= HEAD_DIM) == (u == 1), qt_pair, jnp.zeros_like(qt_pair))
            s_ref[2 * jp + u] = _dot(k_pair, qm)
    for jp in range(C_PAIRS):
        psl = slice(jp * PAIR, (jp + 1) * PAIR)
        v_pair = v_ref[0, pl.ds(kstart, nk), psl]
        outs = []
        for u in range(2):
            h = 2 * jp + u
            bias_rows = []
            for i in range(C_KROWS):
                tiles = [t2_ref[h, jnp.clip(dbase + i - 2 * jq, 0, 2 * C_WIN_H - 1)] for jq in range(C_QROWS // 2)]
                bias_rows.append(jnp.concatenate(tiles, 1))
            s = s_ref[h] + jnp.concatenate(bias_rows, 0) + mask_add
            m = jnp.max(s, 0, keepdims=True)
            p = jnp.exp2(s - m)
            l = jnp.sum(p, 0, keepdims=True)
            outs.append(_dot_tn(v_pair, p.astype(BF16)) / l)
        o_ref[0, psl, :] = jnp.where(feat < HEAD_DIM, outs[0], outs[1]).astype(BF16)


def _natten(qt3, k3, v3, rpb):
    bsz, seq, _ = k3.shape
    rows = seq // GRID_W
    hps = 2 * C_PAIRS
    cols = jnp.arange(GRID_W)
    dc = jnp.clip(cols[:, None] - cols[None, :] + C_WIN_W - 1, 0, 2 * C_WIN_W - 2)
    tt = (rpb.astype(F32) * LOG2E)[:, :, dc]
    zero = jnp.zeros_like(tt[:, :1])
    text = jnp.concatenate([zero, tt, zero], 1)
    t2 = jnp.concatenate([text[:, 1:], text[:, :-1]], -1)
    nd = t2.shape[1]
    nq = C_QROWS * GRID_W
    nk = C_KROWS * GRID_W
    nrb = rows // C_QROWS
    assert nrb >= 3
    masks = jnp.stack([_na_window_mask(r0, rows) for r0 in (0, C_QROWS, rows - C_QROWS)])
    w = C_PAIRS * PAIR
    qblk = pl.BlockSpec((1, w, nq), lambda b, g, r: (b, g, r))
    kblk = pl.BlockSpec((1, seq, w), lambda b, g, r: (b, 0, g))
    return pl.pallas_call(
        functools.partial(_na_kernel, rows=rows),
        grid=(bsz, C_HEADS // hps, nrb),
        in_specs=[qblk, kblk, kblk,
                  pl.BlockSpec((hps, nd, GRID_W, 2 * GRID_W), lambda b, g, r: (g, 0, 0, 0)),
                  pl.BlockSpec((1, nk, nq), lambda b, g, r: ((r > 0).astype(I32) + (r == nrb - 1).astype(I32), 0, 0))],
        out_specs=qblk, out_shape=jax.ShapeDtypeStruct((bsz, C_HEADS * HEAD_DIM, seq), BF16),
        scratch_shapes=[pltpu.VMEM((hps, C_KROWS * GRID_W, nq), F32)],
        compiler_params=_cparams(("parallel", "parallel", "arbitrary")), name="natten")(qt3, k3, v3, t2, masks)


def kernel(x, w_in0, a_sink, mla_q_norm, w_q_up, mla_kv_norm, w_kv_up, w_out0, ln0a_g, ln0a_b,
           router0, w_gate0, w_up0, w_down0, ln0b_g, ln0b_b,
           w_qkv1, na_rpb, w_out1, ln1a_g, ln1a_b,
           router1, w_gate1, w_up1, w_down1, ln1b_g, ln1b_b):
    bsz, seq, d = x.shape
    m = bsz * seq
    x2 = x.reshape(m, d)
    qa, ka, va, qt, km, vt = _even_proj(x2, w_in0, mla_q_norm, w_q_up, mla_kv_norm, w_kv_up, bsz, seq)
    out_a = _swa(qa.reshape(bsz, seq, -1), ka.reshape(bsz, seq, -1), va.reshape(bsz, seq, -1), a_sink)
    out_bt = _mla(qt, km.reshape(bsz, seq, -1), vt)
    h, hb, lg = _outproj([out_a.reshape(m, A_Q)], [w_out0[:A_Q]], [out_bt], [w_out0[A_Q:]],
                         x2, ln0a_g, ln0a_b, router0, bsz, seq)
    h, hb = _moe_block(h, hb, lg, w_gate0, w_up0, w_down0, ln0b_g, ln0b_b, bsz, seq)
    qt1, k2, v2 = _qkv(hb, w_qkv1, bsz, seq)
    na_t = _natten(qt1, k2.reshape(bsz, seq, -1), v2.reshape(bsz, seq, -1), na_rpb)
    h, hb, lg = _outproj([], [], [na_t.reshape(bsz, C_HEADS, HEAD_DIM, seq)], [w_out1],
                         h, ln1a_g, ln1a_b, router1, bsz, seq)
    h, _ = _moe_block(h, hb, lg, w_gate1, w_up1, w_down1, ln1b_g, ln1b_b, bsz, seq)
    return h.reshape(bsz, seq, d)
```
